```python
import math
import jax, jax.numpy as jnp
from jax import lax
import numpy as np

D_MODEL = 2048
BATCH = 4
SEQ = 4096
DEPTH = 4

CHUNK = 64
N_MIXERS = 2
N_A = (DEPTH + 1) // 2
N_B = DEPTH // 2

GM_BLOCK = 128
GM_HALF = 3 * D_MODEL
GM_GROUPS = 8
GM_GROUP_DIM = GM_HALF // GM_GROUPS

MLA_HEADS = 16
Q_RANK = 512
KV_RANK = 512
NOPE_DIM = 128
ROPE_DIM = 64
V_DIM = 128
ROPE_THETA = 10000.0
Q_BLOCK = 128
SM_SCALE = (NOPE_DIM + ROPE_DIM) ** -0.5

D_FF = 5504
CONV_W = 3

ALPHA = (2 * DEPTH) ** 0.25
BETA = (8 * DEPTH) ** -0.25
LN_EPS = 1e-5
RMS_EPS = 1e-6

kernel_name = "interleaved_gmlp_mla_convffn_deepnorm"


def layer_norm(x, g, b):
    xf = x.astype(jnp.float32)
    mu = jnp.mean(xf, axis=-1, keepdims=True)
    var = jnp.mean(jnp.square(xf - mu), axis=-1, keepdims=True)
    y = (xf - mu) * lax.rsqrt(var + LN_EPS)
    return (y * g.astype(jnp.float32) + b.astype(jnp.float32)).astype(x.dtype)


def rms_norm(x, g):
    xf = x.astype(jnp.float32)
    y = xf * lax.rsqrt(jnp.mean(jnp.square(xf), axis=-1, keepdims=True) + RMS_EPS)
    return (y * g.astype(jnp.float32)).astype(x.dtype)


def rope_tables(seq_len):
    half = ROPE_DIM // 2
    inv_freq = ROPE_THETA ** (-jnp.arange(half, dtype=jnp.float32) / half)
    pos = jnp.arange(seq_len, dtype=jnp.float32)
    ang = pos[:, None] * inv_freq[None, :]
    return jnp.cos(ang), jnp.sin(ang)


def apply_rope(x, cos, sin):
    x1, x2 = jnp.split(x, 2, axis=-1)
    cos = cos.astype(x.dtype)
    sin = sin.astype(x.dtype)
    return jnp.concatenate([x1 * cos - x2 * sin, x2 * cos + x1 * sin], axis=-1)


def gmlp_mixer(x, w_in, ln_g, ln_b, w_s, b_s, w_out):
    B, S, _ = x.shape
    z = jax.nn.gelu(x @ w_in)
    u, v = jnp.split(z, 2, axis=-1)
    v = layer_norm(v, ln_g, ln_b)
    v = v.reshape(B, S // GM_BLOCK, GM_BLOCK, GM_GROUPS, GM_GROUP_DIM)
    idx = jnp.arange(GM_BLOCK) // CHUNK
    mask = (idx[None, :] <= idx[:, None]).astype(w_s.dtype)
    w = w_s * mask[None]
    s = jnp.einsum('gij,bnjgd->bnigd', w, v) + jnp.transpose(b_s)[None, None, :, :, None]
    return (u * s.reshape(B, S, GM_HALF)) @ w_out


def chunk_causal_attention(q_nope, q_rope, k_nope, k_rope, v):
    S = q_nope.shape[1]
    outs = []
    for qb in range(S // Q_BLOCK):
        q0 = qb * Q_BLOCK
        k_end = q0 + Q_BLOCK
        s = (jnp.einsum('bqhd,bkhd->bhqk', q_nope[:, q0:k_end], k_nope[:, :k_end])
             + jnp.einsum('bqhr,bkr->bhqk', q_rope[:, q0:k_end], k_rope[:, :k_end]))
        s = s.astype(jnp.float32) * SM_SCALE
        qpos = jnp.arange(q0, k_end)
        kpos = jnp.arange(k_end)
        allowed = (kpos[None, :] // CHUNK) <= (qpos[:, None] // CHUNK)
        s = jnp.where(allowed[None, None], s, -jnp.inf)
        p = jax.nn.softmax(s, axis=-1).astype(v.dtype)
        outs.append(jnp.einsum('bhqk,bkhd->bqhd', p, v[:, :k_end]))
    return jnp.concatenate(outs, axis=1)


def mla_mixer(x, w_in, q_norm_g, kv_norm_g, w_q_b, w_kv_b, w_out, cos, sin):
    B, S, _ = x.shape
    h = x @ w_in
    c_q, c_kv, k_rope = jnp.split(h, [Q_RANK, Q_RANK + KV_RANK], axis=-1)
    q = (rms_norm(c_q, q_norm_g) @ w_q_b).reshape(B, S, MLA_HEADS, NOPE_DIM + ROPE_DIM)
    q_nope, q_rope = jnp.split(q, [NOPE_DIM], axis=-1)
    q_rope = apply_rope(q_rope, cos[:, None, :], sin[:, None, :])
    k_rope = apply_rope(k_rope, cos, sin)
    kv = (rms_norm(c_kv, kv_norm_g) @ w_kv_b).reshape(B, S, MLA_HEADS, NOPE_DIM + V_DIM)
    k_nope, v = jnp.split(kv, [NOPE_DIM], axis=-1)
    o = chunk_causal_attention(q_nope, q_rope, k_nope, k_rope, v)
    return o.reshape(B, S, MLA_HEADS * V_DIM) @ w_out


def conv_ffn(x, w_up, conv_w, conv_b, w_down):
    h = x @ w_up
    C = h.shape[-1]
    h = lax.conv_general_dilated(h, conv_w[:, None, :], window_strides=(1,),
                                 padding=[(CONV_W - 1, 0)],
                                 dimension_numbers=('NWC', 'WIO', 'NWC'),
                                 feature_group_count=C) + conv_b
    a, g = jnp.split(h, 2, axis=-1)
    return (jax.nn.silu(g) * a) @ w_down


def setup_inputs(seed: int = 0) -> dict:
    key = jax.random.key(seed)
    ks = jax.random.split(key, 24)

    def nrm(k, shape, scale):
        return jax.random.normal(k, shape, jnp.float32) * scale

    x = nrm(ks[0], (BATCH, SEQ, D_MODEL), 1.0)
    gm_w_in = nrm(ks[1], (N_A, D_MODEL, 2 * GM_HALF), D_MODEL ** -0.5)
    gm_ln_g = 1.0 + nrm(ks[2], (N_A, GM_HALF), 0.02)
    gm_ln_b = nrm(ks[3], (N_A, GM_HALF), 0.02)
    gm_w_s = nrm(ks[4], (N_A, GM_GROUPS, GM_BLOCK, GM_BLOCK), GM_BLOCK ** -0.5)
    gm_b_s = 1.0 + nrm(ks[5], (N_A, GM_GROUPS, GM_BLOCK), 0.1)
    gm_w_out = nrm(ks[6], (N_A, GM_HALF, D_MODEL), GM_HALF ** -0.5 * BETA)
    mla_w_in = nrm(ks[7], (N_B, D_MODEL, Q_RANK + KV_RANK + ROPE_DIM), D_MODEL ** -0.5)
    mla_q_norm_g = 1.0 + nrm(ks[8], (N_B, Q_RANK), 0.02)
    mla_kv_norm_g = 1.0 + nrm(ks[9], (N_B, KV_RANK), 0.02)
    mla_w_q_b = nrm(ks[10], (N_B, Q_RANK, MLA_HEADS * (NOPE_DIM + ROPE_DIM)), Q_RANK ** -0.5)
    mla_w_kv_b = nrm(ks[11], (N_B, KV_RANK, MLA_HEADS * (NOPE_DIM + V_DIM)), KV_RANK ** -0.5)
    mla_w_out = nrm(ks[12], (N_B, MLA_HEADS * V_DIM, D_MODEL), (MLA_HEADS * V_DIM) ** -0.5 * BETA)
    ffn_w_up = nrm(ks[13], (DEPTH, D_MODEL, 2 * D_FF), D_MODEL ** -0.5)
    ffn_conv_w = nrm(ks[14], (DEPTH, CONV_W, 2 * D_FF), CONV_W ** -0.5)
    ffn_conv_b = nrm(ks[15], (DEPTH, 2 * D_FF), 0.02)
    ffn_w_down = nrm(ks[16], (DEPTH, D_FF, D_MODEL), D_FF ** -0.5 * BETA)
    ln_mix_g = 1.0 + nrm(ks[17], (DEPTH, D_MODEL), 0.02)
    ln_mix_b = nrm(ks[18], (DEPTH, D_MODEL), 0.02)
    ln_ffn_g = 1.0 + nrm(ks[19], (DEPTH, D_MODEL), 0.02)
    ln_ffn_b = nrm(ks[20], (DEPTH, D_MODEL), 0.02)
    return {
        "x": x,
        "gm_w_in": gm_w_in, "gm_ln_g": gm_ln_g, "gm_ln_b": gm_ln_b,
        "gm_w_s": gm_w_s, "gm_b_s": gm_b_s, "gm_w_out": gm_w_out,
        "mla_w_in": mla_w_in, "mla_q_norm_g": mla_q_norm_g, "mla_kv_norm_g": mla_kv_norm_g,
        "mla_w_q_b": mla_w_q_b, "mla_w_kv_b": mla_w_kv_b, "mla_w_out": mla_w_out,
        "ffn_w_up": ffn_w_up, "ffn_conv_w": ffn_conv_w, "ffn_conv_b": ffn_conv_b,
        "ffn_w_down": ffn_w_down,
        "ln_mix_g": ln_mix_g, "ln_mix_b": ln_mix_b, "ln_ffn_g": ln_ffn_g, "ln_ffn_b": ln_ffn_b,
    }


def reference(x, gm_w_in, gm_ln_g, gm_ln_b, gm_w_s, gm_b_s, gm_w_out,
              mla_w_in, mla_q_norm_g, mla_kv_norm_g, mla_w_q_b, mla_w_kv_b, mla_w_out,
              ffn_w_up, ffn_conv_w, ffn_conv_b, ffn_w_down,
              ln_mix_g, ln_mix_b, ln_ffn_g, ln_ffn_b):
    cos, sin = rope_tables(x.shape[1])
    for i in range(DEPTH):
        slot = i // N_MIXERS
        if i % N_MIXERS == 0:
            m = gmlp_mixer(x, gm_w_in[slot], gm_ln_g[slot], gm_ln_b[slot],
                           gm_w_s[slot], gm_b_s[slot], gm_w_out[slot])
        else:
            m = mla_mixer(x, mla_w_in[slot], mla_q_norm_g[slot], mla_kv_norm_g[slot],
                          mla_w_q_b[slot], mla_w_kv_b[slot], mla_w_out[slot], cos, sin)
        x = layer_norm(ALPHA * x + m, ln_mix_g[i], ln_mix_b[i])
        f = conv_ffn(x, ffn_w_up[i], ffn_conv_w[i], ffn_conv_b[i], ffn_w_down[i])
        x = layer_norm(ALPHA * x + f, ln_ffn_g[i], ln_ffn_b[i])
    return x
```

```python
import functools

import jax
import jax.numpy as jnp
from jax import lax
from jax.experimental import pallas as pl
from jax.experimental.pallas import tpu as pltpu

CHUNK = 64
GM_BLOCK = 128
GM_GROUPS = 8
MLA_HEADS = 16
Q_RANK = 512
KV_RANK = 512
NOPE_DIM = 128
ROPE_DIM = 64
V_DIM = 128
ROPE_THETA = 10000.0
SM_SCALE = (NOPE_DIM + ROPE_DIM) ** -0.5
CONV_W = 3
LN_EPS = 1e-5
RMS_EPS = 1e-6

V7X_LANES = 128
V7X_SUBLANES = 8
V7X_VMEM_BYTES = 64 * 1024 * 1024
VMEM_LIMIT_BYTES = V7X_VMEM_BYTES - 8 * 1024 * 1024

QK_PAD = 2 * V7X_LANES

F32 = jnp.float32
BF16 = jnp.bfloat16


def _params(n_axes):
    return pltpu.CompilerParams(
        dimension_semantics=("arbitrary",) * n_axes,
        vmem_limit_bytes=VMEM_LIMIT_BYTES,
    )


def _gelu_tanh(x):
    c = (2.0 / jnp.pi) ** 0.5
    return 0.5 * x * (1.0 + jnp.tanh(c * (x + 0.044715 * (x * x * x))))


def _layer_norm(y, g, b):
    mu = jnp.mean(y, axis=-1, keepdims=True)
    d = y - mu
    var = jnp.mean(d * d, axis=-1, keepdims=True)
    return d * lax.rsqrt(var + LN_EPS) * g + b


def _rms_norm(y, g):
    return y * lax.rsqrt(jnp.mean(y * y, axis=-1, keepdims=True) + RMS_EPS) * g


def _store_residual_ln(x_ref, m, g_ref, b_ref, xo_ref, xob_ref, alpha):
    y = _layer_norm(alpha * x_ref[...] + m, g_ref[...], b_ref[...])
    xo_ref[...] = y
    xob_ref[...] = y.astype(BF16)


def _gmlp_in_kernel(x_ref, w_ref, z_ref):
    h = jnp.dot(x_ref[...], w_ref[...], preferred_element_type=F32)
    z_ref[...] = _gelu_tanh(h).astype(z_ref.dtype)


def _gmlp_in(xb, w_in, tm, tn):
    t, d = xb.shape
    n = w_in.shape[1]
    return pl.pallas_call(
        _gmlp_in_kernel,
        grid=(t // tm, n // tn),
        in_specs=[
            pl.BlockSpec((tm, d), lambda i, j: (i, 0)),
            pl.BlockSpec((d, tn), lambda i, j: (0, j)),
        ],
        out_specs=pl.BlockSpec((tm, tn), lambda i, j: (i, j)),
        out_shape=jax.ShapeDtypeStruct((t, n), BF16),
        compiler_params=_params(2),
        name="gmlp_in",
    )(xb, w_in)


def _sgu_out_kernel(u_ref, v_ref, lng_ref, lnb_ref, ws_ref, bs_ref, wo_ref, x_ref, g_ref, b_ref,
                    xo_ref, xob_ref, acc_ref, p_ref, mu_ref, rs_ref, *, alpha, gd):
    grp = pl.program_id(1)
    tm = u_ref.shape[0]

    @pl.when(grp == 0)
    def _():
        v = v_ref[...].astype(F32)
        mu = jnp.mean(v, axis=-1, keepdims=True)
        d = v - mu
        mu_ref[...] = mu
        rs_ref[...] = lax.rsqrt(jnp.mean(d * d, axis=-1, keepdims=True) + LN_EPS)
        acc_ref[...] = jnp.zeros_like(acc_ref)

    off = pl.multiple_of(grp * gd, V7X_LANES)
    vg = v_ref[:, pl.ds(off, gd)].astype(F32)
    vn = ((vg - mu_ref[...]) * rs_ref[...] * lng_ref[...] + lnb_ref[...]).astype(BF16)

    rows = lax.broadcasted_iota(jnp.int32, (GM_BLOCK, GM_BLOCK), 0) // CHUNK
    cols = lax.broadcasted_iota(jnp.int32, (GM_BLOCK, GM_BLOCK), 1) // CHUNK
    wm = jnp.where(cols <= rows, ws_ref[0], 0.0).astype(BF16)
    bs = bs_ref[0]

    for r in range(tm // GM_BLOCK):
        sl = slice(r * GM_BLOCK, (r + 1) * GM_BLOCK)
        s = jnp.dot(wm, vn[sl, :], preferred_element_type=F32) + bs
        p_ref[sl, :] = (u_ref[sl, :].astype(F32) * s).astype(BF16)
    acc_ref[...] += jnp.dot(p_ref[...], wo_ref[...], preferred_element_type=F32)

    @pl.when(grp == pl.num_programs(1) - 1)
    def _():
        _store_residual_ln(x_ref, acc_ref[...], g_ref, b_ref, xo_ref, xob_ref, alpha)


def _sgu_out(z, ln_g, ln_b, w_s, b_s, w_out, x, g, b, alpha, tm):
    t, two_gh = z.shape
    gh = two_gh // 2
    gd = gh // GM_GROUPS
    d = x.shape[1]
    kern = functools.partial(_sgu_out_kernel, alpha=alpha, gd=gd)
    return pl.pallas_call(
        kern,
        grid=(t // tm, GM_GROUPS),
        in_specs=[
            pl.BlockSpec((tm, gd), lambda i, k: (i, k)),
            pl.BlockSpec((tm, gh), lambda i, k: (i, 1)),
            pl.BlockSpec((1, gd), lambda i, k: (0, k)),
            pl.BlockSpec((1, gd), lambda i, k: (0, k)),
            pl.BlockSpec((1, GM_BLOCK, GM_BLOCK), lambda i, k: (k, 0, 0)),
            pl.BlockSpec((1, GM_BLOCK, 1), lambda i, k: (k, 0, 0)),
            pl.BlockSpec((gd, d), lambda i, k: (k, 0)),
            pl.BlockSpec((tm, d), lambda i, k: (i, 0)),
            pl.BlockSpec((1, d), lambda i, k: (0, 0)),
            pl.BlockSpec((1, d), lambda i, k: (0, 0)),
        ],
        out_specs=[
            pl.BlockSpec((tm, d), lambda i, k: (i, 0)),
            pl.BlockSpec((tm, d), lambda i, k: (i, 0)),
        ],
        out_shape=[jax.ShapeDtypeStruct((t, d), F32), jax.ShapeDtypeStruct((t, d), BF16)],
        scratch_shapes=[
            pltpu.VMEM((tm, d), F32),
            pltpu.VMEM((tm, gd), BF16),
            pltpu.VMEM((tm, 1), F32),
            pltpu.VMEM((tm, 1), F32),
        ],
        compiler_params=_params(2),
        name="sgu_out",
    )(z, z, ln_g, ln_b, w_s, b_s, w_out, x, g, b)


def _ffn_kernel(xb_ref, wu_ref, cw_ref, cb_ref, wd_ref, x_ref, g_ref, b_ref,
                xo_ref, xob_ref, acc_ref, carry_ref, *, alpha, tiles_per_seq):
    i = pl.program_id(0)
    j = pl.program_id(1)
    tm = xb_ref.shape[0]
    tn = wd_ref.shape[0]

    @pl.when(j == 0)
    def _():
        acc_ref[...] = jnp.zeros_like(acc_ref)

    h = jnp.dot(xb_ref[...], wu_ref[...], preferred_element_type=F32)

    first = (i % tiles_per_seq) == 0
    prev = jnp.where(first, 0.0, carry_ref[j])
    carry_ref[j] = h[tm - V7X_SUBLANES:, :]

    head_rows = lax.broadcasted_iota(jnp.int32, (V7X_SUBLANES, 2 * tn), 0)
    cw = cw_ref[...]
    out = cw[2:3, :] * h + cb_ref[...]
    for shift in (1, 2):
        hs = pltpu.roll(h, shift, 0)
        head = jnp.where(head_rows < shift, pltpu.roll(prev, shift, 0), hs[:V7X_SUBLANES, :])
        hs = jnp.concatenate([head, hs[V7X_SUBLANES:, :]], axis=0)
        out = out + cw[2 - shift:3 - shift, :] * hs

    a = out[:, :tn]
    gt = out[:, tn:]
    act = (gt * (1.0 / (1.0 + jnp.exp(-gt))) * a).astype(BF16)
    acc_ref[...] += jnp.dot(act, wd_ref[...], preferred_element_type=F32)

    @pl.when(j == pl.num_programs(1) - 1)
    def _():
        _store_residual_ln(x_ref, acc_ref[...], g_ref, b_ref, xo_ref, xob_ref, alpha)


def _conv_ffn(xb, w_up, conv_w, conv_b, w_down, x, g, b, alpha, seq, tm, tn):
    t, d = x.shape
    n_ff = w_down.shape[0] // tn
    kern = functools.partial(_ffn_kernel, alpha=alpha, tiles_per_seq=seq // tm)
    return pl.pallas_call(
        kern,
        grid=(t // tm, n_ff),
        in_specs=[
            pl.BlockSpec((tm, d), lambda i, j: (i, 0)),
            pl.BlockSpec((d, 2 * tn), lambda i, j: (0, j)),
            pl.BlockSpec((CONV_W, 2 * tn), lambda i, j: (0, j)),
            pl.BlockSpec((1, 2 * tn), lambda i, j: (0, j)),
            pl.BlockSpec((tn, d), lambda i, j: (j, 0)),
            pl.BlockSpec((tm, d), lambda i, j: (i, 0)),
            pl.BlockSpec((1, d), lambda i, j: (0, 0)),
            pl.BlockSpec((1, d), lambda i, j: (0, 0)),
        ],
        out_specs=[
            pl.BlockSpec((tm, d), lambda i, j: (i, 0)),
            pl.BlockSpec((tm, d), lambda i, j: (i, 0)),
        ],
        out_shape=[jax.ShapeDtypeStruct((t, d), F32), jax.ShapeDtypeStruct((t, d), BF16)],
        scratch_shapes=[
            pltpu.VMEM((tm, d), F32),
            pltpu.VMEM((n_ff, V7X_SUBLANES, 2 * tn), F32),
        ],
        compiler_params=_params(2),
        name="conv_ffn",
    )(xb, w_up, conv_w, conv_b, w_down, x, g, b)


def _rope_pad(x, t0, t1, t2):
    half = ROPE_DIM // 2
    return x * t0 + pltpu.roll(x, half, 1) * t1 + pltpu.roll(x, V7X_LANES - half, 1) * t2


def _mla_proj_kernel(xb_ref, win_ref, qg_ref, kvg_ref, wq_ref, wkv_ref, t0_ref, t1_ref, t2_ref,
                     q_ref, kc_ref, v_ref):
    h = jnp.dot(xb_ref[...], win_ref[...], preferred_element_type=F32)
    cq = _rms_norm(h[:, :Q_RANK], qg_ref[...]).astype(BF16)
    ckv = _rms_norm(h[:, Q_RANK:Q_RANK + KV_RANK], kvg_ref[...]).astype(BF16)
    t0, t1, t2 = t0_ref[...], t1_ref[...], t2_ref[...]
    kr = _rope_pad(h[:, Q_RANK + KV_RANK:], t0, t1, t2).astype(BF16)

    q = jnp.dot(cq, wq_ref[...], preferred_element_type=F32)
    kv = jnp.dot(ckv, wkv_ref[...], preferred_element_type=F32)
    for hd in range(MLA_HEADS):
        base = hd * QK_PAD
        q_ref[:, base:base + NOPE_DIM] = q[:, base:base + NOPE_DIM].astype(BF16)
        q_ref[:, base + NOPE_DIM:base + QK_PAD] = _rope_pad(
            q[:, base + NOPE_DIM:base + QK_PAD], t0, t1, t2).astype(BF16)
        kc_ref[:, base:base + NOPE_DIM] = kv[:, base:base + NOPE_DIM].astype(BF16)
        kc_ref[:, base + NOPE_DIM:base + QK_PAD] = kr
        v_ref[:, hd * V_DIM:(hd + 1) * V_DIM] = kv[:, base + NOPE_DIM:base + QK_PAD].astype(BF16)


def _mla_proj(xb, w_in, q_g, kv_g, w_q, w_kv, t0, t1, t2, seq, tm):
    t, d = xb.shape
    n_in = w_in.shape[1]
    tiles_per_seq = seq // tm
    hq = MLA_HEADS * QK_PAD
    const = lambda i: (0, 0)
    rope_spec = pl.BlockSpec((tm, V7X_LANES), lambda i: (i % tiles_per_seq, 0))
    return pl.pallas_call(
        _mla_proj_kernel,
        grid=(t // tm,),
        in_specs=[
            pl.BlockSpec((tm, d), lambda i: (i, 0)),
            pl.BlockSpec((d, n_in), const),
            pl.BlockSpec((1, Q_RANK), const),
            pl.BlockSpec((1, KV_RANK), const),
            pl.BlockSpec((Q_RANK, hq), const),
            pl.BlockSpec((KV_RANK, hq), const),
            rope_spec, rope_spec, rope_spec,
        ],
        out_specs=[
            pl.BlockSpec((tm, hq), lambda i: (i, 0)),
            pl.BlockSpec((tm, hq), lambda i: (i, 0)),
            pl.BlockSpec((tm, MLA_HEADS * V_DIM), lambda i: (i, 0)),
        ],
        out_shape=[
            jax.ShapeDtypeStruct((t, hq), BF16),
            jax.ShapeDtypeStruct((t, hq), BF16),
            jax.ShapeDtypeStruct((t, MLA_HEADS * V_DIM), BF16),
        ],
        compiler_params=_params(1),
        name="mla_proj",
    )(xb, w_in, q_g, kv_g, w_q, w_kv, t0, t1, t2)


def _attn_kernel(q_ref, k_ref, v_ref, o_ref, *, tk):
    qi = pl.program_id(2)
    q = q_ref[...]
    tq = q.shape[0]

    def step(kb, carry, diagonal):
        m, l, acc = carry
        off = pl.multiple_of(kb * tk, tk)
        k = k_ref[pl.ds(off, tk), :]
        v = v_ref[pl.ds(off, tk), :]
        s = lax.dot_general(q, k, (((1,), (1,)), ((), ())), preferred_element_type=F32)
        if diagonal:
            rows = lax.broadcasted_iota(jnp.int32, (tq, tk), 0) // CHUNK
            cols = lax.broadcasted_iota(jnp.int32, (tq, tk), 1) // CHUNK
            s = jnp.where(cols <= rows, s, -jnp.inf)
        m_new = jnp.maximum(m, jnp.max(s, axis=-1, keepdims=True))
        a = jnp.exp((m - m_new) * SM_SCALE)
        p = jnp.exp((s - m_new) * SM_SCALE)
        l = a * l + jnp.sum(p, axis=-1, keepdims=True)
        acc = a * acc + jnp.dot(p.astype(BF16), v, preferred_element_type=F32)
        return m_new, l, acc

    init = (
        jnp.full((tq, 1), -jnp.inf, F32),
        jnp.zeros((tq, 1), F32),
        jnp.zeros((tq, V_DIM), F32),
    )
    carry = lax.fori_loop(0, qi, lambda kb, c: step(kb, c, False), init)
    _, l, acc = step(qi, carry, True)
    o_ref[...] = (acc / l).astype(o_ref.dtype)


def _attention(q, kc, v, batch, seq, tq):
    t = q.shape[0]
    nq = seq // tq
    kern = functools.partial(_attn_kernel, tk=tq)
    return pl.pallas_call(
        kern,
        grid=(batch, MLA_HEADS, nq),
        in_specs=[
            pl.BlockSpec((tq, QK_PAD), lambda b, h, i: (b * nq + i, h)),
            pl.BlockSpec((seq, QK_PAD), lambda b, h, i: (b, h)),
            pl.BlockSpec((seq, V_DIM), lambda b, h, i: (b, h)),
        ],
        out_specs=pl.BlockSpec((tq, V_DIM), lambda b, h, i: (b * nq + i, h)),
        out_shape=jax.ShapeDtypeStruct((t, MLA_HEADS * V_DIM), BF16),
        compiler_params=_params(3),
        name="attention",
    )(q, kc, v)


def _attn_out_kernel(o_ref, w_ref, x_ref, g_ref, b_ref, xo_ref, xob_ref, *, alpha):
    m = jnp.dot(o_ref[...], w_ref[...], preferred_element_type=F32)
    _store_residual_ln(x_ref, m, g_ref, b_ref, xo_ref, xob_ref, alpha)


def _attn_out(o, w_out, x, g, b, alpha, tm):
    t, d = x.shape
    k = o.shape[1]
    kern = functools.partial(_attn_out_kernel, alpha=alpha)
    return pl.pallas_call(
        kern,
        grid=(t // tm,),
        in_specs=[
            pl.BlockSpec((tm, k), lambda i: (i, 0)),
            pl.BlockSpec((k, d), lambda i: (0, 0)),
            pl.BlockSpec((tm, d), lambda i: (i, 0)),
            pl.BlockSpec((1, d), lambda i: (0, 0)),
            pl.BlockSpec((1, d), lambda i: (0, 0)),
        ],
        out_specs=[
            pl.BlockSpec((tm, d), lambda i: (i, 0)),
            pl.BlockSpec((tm, d), lambda i: (i, 0)),
        ],
        out_shape=[jax.ShapeDtypeStruct((t, d), F32), jax.ShapeDtypeStruct((t, d), BF16)],
        compiler_params=_params(1),
        name="attn_out",
    )(o, w_out, x, g, b)


def _round_up(n, m):
    return (n + m - 1) // m * m


def _interleave_halves(w, d_ff, tn):
    ffp = _round_up(d_ff, tn)
    lead = w.shape[:-1]
    w = w.reshape(lead + (2, d_ff))
    w = jnp.pad(w, [(0, 0)] * len(lead) + [(0, 0), (0, ffp - d_ff)])
    w = w.reshape(lead + (2, ffp // tn, tn))
    w = jnp.swapaxes(w, -3, -2)
    return w.reshape(lead + (2 * ffp,))


def _rope_tables(seq):
    half = ROPE_DIM // 2
    inv_freq = ROPE_THETA ** (-jnp.arange(half, dtype=F32) / half)
    ang = jnp.arange(seq, dtype=F32)[:, None] * inv_freq[None, :]
    cos, sin = jnp.cos(ang), jnp.sin(ang)
    z = jnp.zeros_like(cos)
    zz = jnp.zeros((seq, V7X_LANES - ROPE_DIM), F32)
    t0 = jnp.concatenate([cos, cos, zz], axis=1)
    t1 = jnp.concatenate([z, sin, zz], axis=1)
    t2 = jnp.concatenate([-sin, z, zz], axis=1)
    return t0, t1, t2


def _pad_heads(w, width):
    k = w.shape[0]
    w = w.reshape(k, MLA_HEADS, width)
    w = jnp.pad(w, ((0, 0), (0, 0), (0, QK_PAD - width)))
    return w.reshape(k, MLA_HEADS * QK_PAD)


def kernel(x, gm_w_in, gm_ln_g, gm_ln_b, gm_w_s, gm_b_s, gm_w_out, mla_w_in, mla_q_norm_g, mla_kv_norm_g, mla_w_q_b, mla_w_kv_b, mla_w_out, ffn_w_up, ffn_conv_w, ffn_conv_b, ffn_w_down, ln_mix_g, ln_mix_b, ln_ffn_g, ln_ffn_b):
    batch, seq, d = x.shape
    t = batch * seq
    depth = ffn_w_up.shape[0]
    d_ff = ffn_w_down.shape[1]
    alpha = float((2 * depth) ** 0.25)

    tm = min(512, seq)
    tm_in = min(1024, seq)
    tn_in = 1024
    tn_ff = 512
    tq = min(512, seq)
    assert seq % tm == 0 and seq % tm_in == 0 and seq % tq == 0 and tm % GM_BLOCK == 0
    assert (2 * gm_w_out.shape[1]) % tn_in == 0 and tq % CHUNK == 0

    xf = x.reshape(t, d)
    xb = xf.astype(BF16)
    t0, t1, t2 = _rope_tables(seq)
    ffp = _round_up(d_ff, tn_ff)

    for i in range(depth):
        slot = i // 2
        g_mix, b_mix = ln_mix_g[i][None, :], ln_mix_b[i][None, :]
        if i % 2 == 0:
            z = _gmlp_in(xb, gm_w_in[slot].astype(BF16), tm_in, tn_in)
            xf, xb = _sgu_out(
                z, gm_ln_g[slot][None, :], gm_ln_b[slot][None, :], gm_w_s[slot],
                gm_b_s[slot][:, :, None], gm_w_out[slot].astype(BF16), xf, g_mix, b_mix, alpha, tm)
        else:
            w_in = jnp.pad(mla_w_in[slot], ((0, 0), (0, V7X_LANES - ROPE_DIM))).astype(BF16)
            w_q = _pad_heads(mla_w_q_b[slot], NOPE_DIM + ROPE_DIM).astype(BF16)
            q, kc, v = _mla_proj(
                xb, w_in, mla_q_norm_g[slot][None, :], mla_kv_norm_g[slot][None, :],
                w_q, mla_w_kv_b[slot].astype(BF16), t0, t1, t2, seq, tm)
            o = _attention(q, kc, v, batch, seq, tq)
            xf, xb = _attn_out(o, mla_w_out[slot].astype(BF16), xf, g_mix, b_mix, alpha, tm)

        w_up = _interleave_halves(ffn_w_up[i], d_ff, tn_ff).astype(BF16)
        conv_w = _interleave_halves(ffn_conv_w[i], d_ff, tn_ff)
        conv_b = _interleave_halves(ffn_conv_b[i][None, :], d_ff, tn_ff)
        w_down = jnp.pad(ffn_w_down[i], ((0, ffp - d_ff), (0, 0))).astype(BF16)
        xf, xb = _conv_ffn(xb, w_up, conv_w, conv_b, w_down, xf,
                           ln_ffn_g[i][None, :], ln_ffn_b[i][None, :], alpha, seq, tm, tn_ff)

    return xf.reshape(batch, seq, d)
```

```python
import functools

import jax
import jax.numpy as jnp
from jax import lax
from jax.experimental import pallas as pl
from jax.experimental.pallas import tpu as pltpu

CHUNK = 64
GM_BLOCK = 128
GM_GROUPS = 8
MLA_HEADS = 16
Q_RANK = 512
KV_RANK = 512
NOPE_DIM = 128
ROPE_DIM = 64
V_DIM = 128
ROPE_THETA = 10000.0
SM_SCALE = (NOPE_DIM + ROPE_DIM) ** -0.5
LOG2_E = 1.4426950408889634
CONV_W = 3
LN_EPS = 1e-5
RMS_EPS = 1e-6

V7X_LANES = 128
V7X_SUBLANES = 8
V7X_VMEM_BYTES = 64 * 1024 * 1024
VMEM_LIMIT_BYTES = V7X_VMEM_BYTES - 8 * 1024 * 1024

QK_PAD = 2 * V7X_LANES
KV_WIDTH = NOPE_DIM + V_DIM
ATTN_HEADS_PER_STEP = 2

F32 = jnp.float32
BF16 = jnp.bfloat16


def _params(n_axes):
    return pltpu.CompilerParams(
        dimension_semantics=("arbitrary",) * n_axes,
        vmem_limit_bytes=VMEM_LIMIT_BYTES,
    )


def _gelu_tanh(x):
    c = (2.0 / jnp.pi) ** 0.5
    return 0.5 * x * (1.0 + jnp.tanh(c * (x + 0.044715 * (x * x * x))))


def _layer_norm(y, g, b):
    mu = jnp.mean(y, axis=-1, keepdims=True)
    d = y - mu
    var = jnp.mean(d * d, axis=-1, keepdims=True)
    return d * lax.rsqrt(var + LN_EPS) * g + b


def _rms_norm(y, g):
    return y * lax.rsqrt(jnp.mean(y * y, axis=-1, keepdims=True) + RMS_EPS) * g


def _store_residual_ln(x_ref, m, g_ref, b_ref, xo_ref, xob_ref, alpha):
    y = _layer_norm(alpha * x_ref[...] + m, g_ref[...], b_ref[...])
    xo_ref[...] = y
    xob_ref[...] = y.astype(BF16)


def _cast_kernel(w_ref, o_ref, *, segments, n_in_blocks, n_out_blocks):
    def copy():
        for dst, src, width in segments:
            if src is None:
                o_ref[:, dst:dst + width] = jnp.zeros((o_ref.shape[0], width), o_ref.dtype)
            else:
                o_ref[:, dst:dst + width] = w_ref[:, src:src + width].astype(o_ref.dtype)

    if n_in_blocks == n_out_blocks:
        copy()
    else:
        r = pl.program_id(1)
        pl.when(r < n_in_blocks)(copy)

        @pl.when(r >= n_in_blocks)
        def _():
            o_ref[...] = jnp.zeros_like(o_ref)


def _cast_weights(w, tr, segments=None, n_out=None, k_out=None):
    n_layers, k, n = w.shape
    n_out = n if n_out is None else n_out
    k_out = k if k_out is None else k_out
    segments = ((0, 0, n),) if segments is None else tuple(segments)
    assert k % tr == 0 and k_out % tr == 0
    n_in_blocks, n_out_blocks = k // tr, k_out // tr
    kern = functools.partial(_cast_kernel, segments=segments, n_in_blocks=n_in_blocks,
                             n_out_blocks=n_out_blocks)
    return pl.pallas_call(
        kern,
        grid=(n_layers, n_out_blocks),
        in_specs=[pl.BlockSpec((None, tr, n), lambda l, r: (l, jnp.minimum(r, n_in_blocks - 1), 0))],
        out_specs=pl.BlockSpec((None, tr, n_out), lambda l, r: (l, r, 0)),
        out_shape=jax.ShapeDtypeStruct((n_layers, k_out, n_out), BF16),
        compiler_params=_params(2),
        name="cast_weights",
    )(w)


def _gmlp_in_kernel(x_ref, w_ref, z_ref):
    h = jnp.dot(x_ref[...], w_ref[...], preferred_element_type=F32)
    z_ref[...] = _gelu_tanh(h).astype(z_ref.dtype)


def _gmlp_in(xb, w_in, slot, tm, tn):
    t, d = xb.shape
    n = w_in.shape[2]
    return pl.pallas_call(
        _gmlp_in_kernel,
        grid=(t // tm, n // tn),
        in_specs=[
            pl.BlockSpec((tm, d), lambda i, j: (i, 0)),
            pl.BlockSpec((None, d, tn), lambda i, j: (slot, 0, j)),
        ],
        out_specs=pl.BlockSpec((tm, tn), lambda i, j: (i, j)),
        out_shape=jax.ShapeDtypeStruct((t, n), BF16),
        compiler_params=_params(2),
        name="gmlp_in",
    )(xb, w_in)


def _sgu_out_kernel(u_ref, v_ref, lng_ref, lnb_ref, ws_ref, bs_ref, wo_ref, x_ref, g_ref, b_ref,
                    xo_ref, xob_ref, acc_ref, p_ref, mu_ref, rs_ref, *, alpha, gd):
    grp = pl.program_id(1)
    tm = u_ref.shape[0]

    @pl.when(grp == 0)
    def _():
        v = v_ref[...].astype(F32)
        mu = jnp.mean(v, axis=-1, keepdims=True)
        d = v - mu
        mu_ref[...] = mu
        rs_ref[...] = lax.rsqrt(jnp.mean(d * d, axis=-1, keepdims=True) + LN_EPS)
        acc_ref[...] = jnp.zeros_like(acc_ref)

    off = pl.multiple_of(grp * gd, V7X_LANES)
    vg = v_ref[:, pl.ds(off, gd)].astype(F32)
    vn = ((vg - mu_ref[...]) * rs_ref[...] * lng_ref[...] + lnb_ref[...]).astype(BF16)

    rows = lax.broadcasted_iota(jnp.int32, (GM_BLOCK, GM_BLOCK), 0) // CHUNK
    cols = lax.broadcasted_iota(jnp.int32, (GM_BLOCK, GM_BLOCK), 1) // CHUNK
    wm = jnp.where(cols <= rows, ws_ref[0], 0.0).astype(BF16)
    bs = bs_ref[0]

    for r in range(tm // GM_BLOCK):
        sl = slice(r * GM_BLOCK, (r + 1) * GM_BLOCK)
        s = jnp.dot(wm, vn[sl, :], preferred_element_type=F32) + bs
        p_ref[sl, :] = (u_ref[sl, :].astype(F32) * s).astype(BF16)
    acc_ref[...] += jnp.dot(p_ref[...], wo_ref[...], preferred_element_type=F32)

    @pl.when(grp == pl.num_programs(1) - 1)
    def _():
        _store_residual_ln(x_ref, acc_ref[...], g_ref, b_ref, xo_ref, xob_ref, alpha)


def _sgu_out(z, ln_g, ln_b, w_s, b_s, w_out, x, g, b, slot, layer, alpha, tm):
    t, two_gh = z.shape
    gh = two_gh // 2
    gd = gh // GM_GROUPS
    d = x.shape[1]
    kern = functools.partial(_sgu_out_kernel, alpha=alpha, gd=gd)
    return pl.pallas_call(
        kern,
        grid=(t // tm, GM_GROUPS),
        in_specs=[
            pl.BlockSpec((tm, gd), lambda i, k: (i, k)),
            pl.BlockSpec((tm, gh), lambda i, k: (i, 1)),
            pl.BlockSpec((None, 1, gd), lambda i, k: (slot, 0, k)),
            pl.BlockSpec((None, 1, gd), lambda i, k: (slot, 0, k)),
            pl.BlockSpec((None, 1, GM_BLOCK, GM_BLOCK), lambda i, k: (slot, k, 0, 0)),
            pl.BlockSpec((None, 1, GM_BLOCK, 1), lambda i, k: (slot, k, 0, 0)),
            pl.BlockSpec((None, gd, d), lambda i, k: (slot, k, 0)),
            pl.BlockSpec((tm, d), lambda i, k: (i, 0)),
            pl.BlockSpec((None, 1, d), lambda i, k: (layer, 0, 0)),
            pl.BlockSpec((None, 1, d), lambda i, k: (layer, 0, 0)),
        ],
        out_specs=[
            pl.BlockSpec((tm, d), lambda i, k: (i, 0)),
            pl.BlockSpec((tm, d), lambda i, k: (i, 0)),
        ],
        out_shape=[jax.ShapeDtypeStruct((t, d), F32), jax.ShapeDtypeStruct((t, d), BF16)],
        scratch_shapes=[
            pltpu.VMEM((tm, d), F32),
            pltpu.VMEM((tm, gd), BF16),
            pltpu.VMEM((tm, 1), F32),
            pltpu.VMEM((tm, 1), F32),
        ],
        compiler_params=_params(2),
        name="sgu_out",
    )(z, z, ln_g, ln_b, w_s, b_s, w_out, x, g, b)


def _ffn_kernel(xb_ref, wu_ref, cw_ref, cb_ref, wd_ref, x_ref, g_ref, b_ref,
                xo_ref, xob_ref, acc_ref, h0_ref, h1_ref, carry_ref, *, alpha, tiles_per_seq, n_ff):
    s = pl.program_id(0)

    @pl.when(s == 0)
    def _():
        h1_ref[...] = jnp.zeros_like(h1_ref)
        acc_ref[...] = jnp.zeros_like(acc_ref)
        carry_ref[...] = jnp.zeros_like(carry_ref)

    stage = functools.partial(
        _ffn_stage, xb_ref, wu_ref, cw_ref, cb_ref, wd_ref, x_ref, g_ref, b_ref, xo_ref, xob_ref,
        acc_ref, carry_ref, alpha=alpha, tiles_per_seq=tiles_per_seq, n_ff=n_ff)
    pl.when(s % 2 == 0)(functools.partial(stage, h1_ref, h0_ref))
    pl.when(s % 2 == 1)(functools.partial(stage, h0_ref, h1_ref))


def _ffn_stage(xb_ref, wu_ref, cw_ref, cb_ref, wd_ref, x_ref, g_ref, b_ref, xo_ref, xob_ref,
               acc_ref, carry_ref, h_rd_ref, h_wr_ref, *, alpha, tiles_per_seq, n_ff):
    s = pl.program_id(0)
    tm = xb_ref.shape[0]
    tn = wd_ref.shape[0]
    lag = jnp.maximum(s - 1, 0)
    ip = lag // n_ff
    jp = lag % n_ff

    h_wr_ref[...] = jnp.dot(xb_ref[...], wu_ref[...], preferred_element_type=F32)

    h = h_rd_ref[...]

    first = (ip % tiles_per_seq) == 0
    prev = jnp.where(first, 0.0, carry_ref[jp])
    carry_ref[jp] = h[tm - V7X_SUBLANES:, :]

    head_rows = lax.broadcasted_iota(jnp.int32, (V7X_SUBLANES, 2 * tn), 0)
    cw = cw_ref[...]
    out = cw[2:3, :] * h + cb_ref[...]
    for shift in (1, 2):
        hs = pltpu.roll(h, shift, 0)
        head = jnp.where(head_rows < shift, pltpu.roll(prev, shift, 0), hs[:V7X_SUBLANES, :])
        hs = jnp.concatenate([head, hs[V7X_SUBLANES:, :]], axis=0)
        out = out + cw[2 - shift:3 - shift, :] * hs

    a = out[:, :tn]
    gt = out[:, tn:]
    act = gt * (1.0 / (1.0 + jnp.exp(-gt))) * a
    act = jnp.where(s > 0, act, 0.0).astype(BF16)
    acc_ref[...] += jnp.dot(act, wd_ref[...], preferred_element_type=F32)

    @pl.when((s > 0) & (jp == n_ff - 1))
    def _():
        _store_residual_ln(x_ref, acc_ref[...], g_ref, b_ref, xo_ref, xob_ref, alpha)
        acc_ref[...] = jnp.zeros_like(acc_ref)


def _conv_ffn(xb, w_up, conv_w, conv_b, w_down, x, g, b, layer, alpha, seq, tm, tn):
    t, d = x.shape
    n_ff = w_down.shape[1] // tn
    n_steps = (t // tm) * n_ff
    kern = functools.partial(_ffn_kernel, alpha=alpha, tiles_per_seq=seq // tm, n_ff=n_ff)

    def cur(s):
        c = jnp.minimum(s, n_steps - 1)
        return c // n_ff, c % n_ff

    def lag(s):
        c = jnp.maximum(s - 1, 0)
        return c // n_ff, c % n_ff

    return pl.pallas_call(
        kern,
        grid=(n_steps + 1,),
        in_specs=[
            pl.BlockSpec((tm, d), lambda s: (cur(s)[0], 0)),
            pl.BlockSpec((None, d, 2 * tn), lambda s: (layer, 0, cur(s)[1])),
            pl.BlockSpec((None, CONV_W, 2 * tn), lambda s: (layer, 0, lag(s)[1])),
            pl.BlockSpec((None, 1, 2 * tn), lambda s: (layer, 0, lag(s)[1])),
            pl.BlockSpec((None, tn, d), lambda s: (layer, lag(s)[1], 0)),
            pl.BlockSpec((tm, d), lambda s: (lag(s)[0], 0)),
            pl.BlockSpec((None, 1, d), lambda s: (layer, 0, 0)),
            pl.BlockSpec((None, 1, d), lambda s: (layer, 0, 0)),
        ],
        out_specs=[
            pl.BlockSpec((tm, d), lambda s: (lag(s)[0], 0)),
            pl.BlockSpec((tm, d), lambda s: (lag(s)[0], 0)),
        ],
        out_shape=[jax.ShapeDtypeStruct((t, d), F32), jax.ShapeDtypeStruct((t, d), BF16)],
        scratch_shapes=[
            pltpu.VMEM((tm, d), F32),
            pltpu.VMEM((tm, 2 * tn), F32),
            pltpu.VMEM((tm, 2 * tn), F32),
            pltpu.VMEM((n_ff, V7X_SUBLANES, 2 * tn), F32),
        ],
        compiler_params=_params(1),
        name="conv_ffn",
    )(xb, w_up, conv_w, conv_b, w_down, x, g, b)


def _rope_pad(x, t0, t1, t2):
    half = ROPE_DIM // 2
    return x * t0 + pltpu.roll(x, half, 1) * t1 + pltpu.roll(x, V7X_LANES - half, 1) * t2


def _mla_proj_kernel(xb_ref, win_ref, qg_ref, kvg_ref, wq_ref, wkv_ref, t0_ref, t1_ref, t2_ref,
                     q_ref, kc_ref, v_ref):
    h = jnp.dot(xb_ref[...], win_ref[...], preferred_element_type=F32)
    cq = _rms_norm(h[:, :Q_RANK], qg_ref[...]).astype(BF16)
    ckv = _rms_norm(h[:, Q_RANK:Q_RANK + KV_RANK], kvg_ref[...]).astype(BF16)
    t0, t1, t2 = t0_ref[...], t1_ref[...], t2_ref[...]
    kr = _rope_pad(h[:, Q_RANK + KV_RANK:], t0, t1, t2).astype(BF16)

    q = jnp.dot(cq, wq_ref[...], preferred_element_type=F32)
    kv = jnp.dot(ckv, wkv_ref[...], preferred_element_type=F32)
    for hd in range(MLA_HEADS):
        qb = hd * QK_PAD
        kb = hd * KV_WIDTH
        q_ref[:, qb:qb + NOPE_DIM] = q[:, qb:qb + NOPE_DIM].astype(BF16)
        q_ref[:, qb + NOPE_DIM:qb + QK_PAD] = _rope_pad(
            q[:, qb + NOPE_DIM:qb + QK_PAD], t0, t1, t2).astype(BF16)
        kc_ref[:, qb:qb + NOPE_DIM] = kv[:, kb:kb + NOPE_DIM].astype(BF16)
        kc_ref[:, qb + NOPE_DIM:qb + QK_PAD] = kr
        v_ref[:, hd * V_DIM:(hd + 1) * V_DIM] = kv[:, kb + NOPE_DIM:kb + KV_WIDTH].astype(BF16)


def _mla_proj(xb, w_in, q_g, kv_g, w_q, w_kv, t0, t1, t2, slot, seq, tm):
    t, d = xb.shape
    n_in = w_in.shape[1]
    tiles_per_seq = seq // tm
    hq = MLA_HEADS * QK_PAD
    rope_spec = pl.BlockSpec((tm, V7X_LANES), lambda i: (i % tiles_per_seq, 0))
    return pl.pallas_call(
        _mla_proj_kernel,
        grid=(t // tm,),
        in_specs=[
            pl.BlockSpec((tm, d), lambda i: (i, 0)),
            pl.BlockSpec((d, n_in), lambda i: (0, 0)),
            pl.BlockSpec((None, 1, Q_RANK), lambda i: (slot, 0, 0)),
            pl.BlockSpec((None, 1, KV_RANK), lambda i: (slot, 0, 0)),
            pl.BlockSpec((Q_RANK, hq), lambda i: (0, 0)),
            pl.BlockSpec((None, KV_RANK, MLA_HEADS * KV_WIDTH), lambda i: (slot, 0, 0)),
            rope_spec, rope_spec, rope_spec,
        ],
        out_specs=[
            pl.BlockSpec((tm, hq), lambda i: (i, 0)),
            pl.BlockSpec((tm, hq), lambda i: (i, 0)),
            pl.BlockSpec((tm, MLA_HEADS * V_DIM), lambda i: (i, 0)),
        ],
        out_shape=[
            jax.ShapeDtypeStruct((t, hq), BF16),
            jax.ShapeDtypeStruct((t, hq), BF16),
            jax.ShapeDtypeStruct((t, MLA_HEADS * V_DIM), BF16),
        ],
        compiler_params=_params(1),
        name="mla_proj",
    )(xb, w_in, q_g, kv_g, w_q, w_kv, t0, t1, t2)


def _attn_kernel(q_ref, k_ref, v_ref, o_ref, *, tk):
    qi = pl.program_id(2)
    tq = q_ref.shape[0]
    c = SM_SCALE * LOG2_E
    qs = [q_ref[:, hd * QK_PAD:(hd + 1) * QK_PAD] for hd in range(ATTN_HEADS_PER_STEP)]

    def step(kb, carry, diagonal):
        off = pl.multiple_of(kb * tk, tk)
        new = []
        for hd in range(ATTN_HEADS_PER_STEP):
            m, l, acc = carry[hd]
            k = k_ref[pl.ds(off, tk), hd * QK_PAD:(hd + 1) * QK_PAD]
            v = v_ref[pl.ds(off, tk), hd * V_DIM:(hd + 1) * V_DIM]
            s = lax.dot_general(qs[hd], k, (((1,), (1,)), ((), ())), preferred_element_type=F32)
            if diagonal:
                rows = lax.broadcasted_iota(jnp.int32, (tq, tk), 0) // CHUNK
                cols = lax.broadcasted_iota(jnp.int32, (tq, tk), 1) // CHUNK
                s = jnp.where(cols <= rows, s, -jnp.inf)
            m_new = jnp.maximum(m, jnp.max(s, axis=-1, keepdims=True))
            a = jnp.exp2((m - m_new) * c)
            p = jnp.exp2((s - m_new) * c)
            l = a * l + jnp.sum(p, axis=-1, keepdims=True)
            acc = a * acc + jnp.dot(p.astype(BF16), v, preferred_element_type=F32)
            new.append((m_new, l, acc))
        return tuple(new)

    init = tuple(
        (jnp.full((tq, 1), -jnp.inf, F32), jnp.zeros((tq, 1), F32), jnp.zeros((tq, V_DIM), F32))
        for _ in range(ATTN_HEADS_PER_STEP))
    carry = lax.fori_loop(0, qi, lambda kb, cr: step(kb, cr, False), init)
    carry = step(qi, carry, True)
    for hd in range(ATTN_HEADS_PER_STEP):
        _, l, acc = carry[hd]
        o_ref[:, hd * V_DIM:(hd + 1) * V_DIM] = (acc / l).astype(o_ref.dtype)


def _attention(q, kc, v, batch, seq, tq):
    t = q.shape[0]
    nq = seq // tq
    hps = ATTN_HEADS_PER_STEP
    kern = functools.partial(_attn_kernel, tk=tq)
    return pl.pallas_call(
        kern,
        grid=(batch, MLA_HEADS // hps, nq),
        in_specs=[
            pl.BlockSpec((tq, hps * QK_PAD), lambda b, h, i: (b * nq + i, h)),
            pl.BlockSpec((seq, hps * QK_PAD), lambda b, h, i: (b, h)),
            pl.BlockSpec((seq, hps * V_DIM), lambda b, h, i: (b, h)),
        ],
        out_specs=pl.BlockSpec((tq, hps * V_DIM), lambda b, h, i: (b * nq + i, h)),
        out_shape=jax.ShapeDtypeStruct((t, MLA_HEADS * V_DIM), BF16),
        compiler_params=_params(3),
        name="attention",
    )(q, kc, v)


def _attn_out_kernel(o_ref, w_ref, x_ref, g_ref, b_ref, xo_ref, xob_ref, *, alpha):
    m = jnp.dot(o_ref[...], w_ref[...], preferred_element_type=F32)
    _store_residual_ln(x_ref, m, g_ref, b_ref, xo_ref, xob_ref, alpha)


def _attn_out(o, w_out, x, g, b, slot, layer, alpha, tm):
    t, d = x.shape
    k = o.shape[1]
    kern = functools.partial(_attn_out_kernel, alpha=alpha)
    return pl.pallas_call(
        kern,
        grid=(t // tm,),
        in_specs=[
            pl.BlockSpec((tm, k), lambda i: (i, 0)),
            pl.BlockSpec((None, k, d), lambda i: (slot, 0, 0)),
            pl.BlockSpec((tm, d), lambda i: (i, 0)),
            pl.BlockSpec((None, 1, d), lambda i: (layer, 0, 0)),
            pl.BlockSpec((None, 1, d), lambda i: (layer, 0, 0)),
        ],
        out_specs=[
            pl.BlockSpec((tm, d), lambda i: (i, 0)),
            pl.BlockSpec((tm, d), lambda i: (i, 0)),
        ],
        out_shape=[jax.ShapeDtypeStruct((t, d), F32), jax.ShapeDtypeStruct((t, d), BF16)],
        compiler_params=_params(1),
        name="attn_out",
    )(o, w_out, x, g, b)


def _round_up(n, m):
    return (n + m - 1) // m * m


def _ffn_segments(d_ff, tn):
    segs = []
    for j in range(_round_up(d_ff, tn) // tn):
        width = min(tn, d_ff - j * tn)
        for half in range(2):
            dst = (2 * j + half) * tn
            segs.append((dst, half * d_ff + j * tn, width))
            if width < tn:
                segs.append((dst + width, None, tn - width))
    return segs


def _interleave_halves(w, d_ff, tn):
    parts = []
    for dst, src, width in _ffn_segments(d_ff, tn):
        parts.append(jnp.zeros(w.shape[:-1] + (width,), w.dtype) if src is None
                     else w[..., src:src + width])
    return jnp.concatenate(parts, axis=-1)


def _rope_tables(seq):
    half = ROPE_DIM // 2
    inv_freq = ROPE_THETA ** (-jnp.arange(half, dtype=F32) / half)
    ang = jnp.arange(seq, dtype=F32)[:, None] * inv_freq[None, :]
    cos, sin = jnp.cos(ang), jnp.sin(ang)
    z = jnp.zeros_like(cos)
    zz = jnp.zeros((seq, V7X_LANES - ROPE_DIM), F32)
    t0 = jnp.concatenate([cos, cos, zz], axis=1)
    t1 = jnp.concatenate([z, sin, zz], axis=1)
    t2 = jnp.concatenate([-sin, z, zz], axis=1)
    return t0, t1, t2


def _pad_heads(w, width):
    k = w.shape[0]
    w = w.reshape(k, MLA_HEADS, width)
    w = jnp.pad(w, ((0, 0), (0, 0), (0, QK_PAD - width)))
    return w.reshape(k, MLA_HEADS * QK_PAD)


def kernel(x, gm_w_in, gm_ln_g, gm_ln_b, gm_w_s, gm_b_s, gm_w_out, mla_w_in, mla_q_norm_g, mla_kv_norm_g, mla_w_q_b, mla_w_kv_b, mla_w_out, ffn_w_up, ffn_conv_w, ffn_conv_b, ffn_w_down, ln_mix_g, ln_mix_b, ln_ffn_g, ln_ffn_b):
    batch, seq, d = x.shape
    t = batch * seq
    depth = ffn_w_up.shape[0]
    d_ff = ffn_w_down.shape[1]
    alpha = float((2 * depth) ** 0.25)

    tm = min(512, seq)
    tm_in = min(1024, seq)
    tn_in = 1024
    tn_ff = 512
    tq = min(512, seq)
    assert seq % tm == 0 and seq % tm_in == 0 and seq % tq == 0 and tm % GM_BLOCK == 0
    assert (2 * gm_w_out.shape[1]) % tn_in == 0 and tq % CHUNK == 0
    ffp = _round_up(d_ff, tn_ff)

    gm_w_in_b = _cast_weights(gm_w_in, tr=256)
    gm_w_out_b = _cast_weights(gm_w_out, tr=1024)
    mla_w_kv_b = _cast_weights(mla_w_kv_b, tr=KV_RANK)
    mla_w_out_b = _cast_weights(mla_w_out, tr=1024)
    ffn_w_up_b = _cast_weights(ffn_w_up, tr=256, segments=_ffn_segments(d_ff, tn_ff), n_out=2 * ffp)
    ffn_w_down_b = _cast_weights(ffn_w_down, tr=V7X_LANES, k_out=ffp)
    conv_w = _interleave_halves(ffn_conv_w, d_ff, tn_ff)
    conv_b = _interleave_halves(ffn_conv_b[:, None, :], d_ff, tn_ff)

    gm_ln_g3, gm_ln_b3 = gm_ln_g[:, None, :], gm_ln_b[:, None, :]
    gm_b_s4 = gm_b_s[:, :, :, None]
    q_g3, kv_g3 = mla_q_norm_g[:, None, :], mla_kv_norm_g[:, None, :]
    mix_g3, mix_b3 = ln_mix_g[:, None, :], ln_mix_b[:, None, :]
    ffn_g3, ffn_b3 = ln_ffn_g[:, None, :], ln_ffn_b[:, None, :]

    xf = x.reshape(t, d)
    xb = xf.astype(BF16)
    t0, t1, t2 = _rope_tables(seq)

    for i in range(depth):
        slot = i // 2
        if i % 2 == 0:
            z = _gmlp_in(xb, gm_w_in_b, slot, tm_in, tn_in)
            xf, xb = _sgu_out(z, gm_ln_g3, gm_ln_b3, gm_w_s, gm_b_s4, gm_w_out_b, xf,
                              mix_g3, mix_b3, slot, i, alpha, tm)
        else:
            w_in = jnp.pad(mla_w_in[slot], ((0, 0), (0, V7X_LANES - ROPE_DIM))).astype(BF16)
            w_q = _pad_heads(mla_w_q_b[slot], NOPE_DIM + ROPE_DIM).astype(BF16)
            q, kc, v = _mla_proj(xb, w_in, q_g3, kv_g3, w_q, mla_w_kv_b, t0, t1, t2, slot, seq, tm)
            o = _attention(q, kc, v, batch, seq, tq)
            xf, xb = _attn_out(o, mla_w_out_b, xf, mix_g3, mix_b3, slot, i, alpha, tm)

        xf, xb = _conv_ffn(xb, ffn_w_up_b, conv_w, conv_b, ffn_w_down_b, xf, ffn_g3, ffn_b3,
                           i, alpha, seq, tm, tn_ff)

    return xf.reshape(batch, seq, d)
```

```python
import functools

import jax
import jax.numpy as jnp
from jax import lax
from jax.experimental import pallas as pl
from jax.experimental.pallas import tpu as pltpu

CHUNK = 64
GM_BLOCK = 128
GM_GROUPS = 8
MLA_HEADS = 16
Q_RANK = 512
KV_RANK = 512
NOPE_DIM = 128
ROPE_DIM = 64
V_DIM = 128
ROPE_THETA = 10000.0
SM_SCALE = (NOPE_DIM + ROPE_DIM) ** -0.5
LOG2_E = 1.4426950408889634
CONV_W = 3
LN_EPS = 1e-5
RMS_EPS = 1e-6

V7X_LANES = 128
V7X_SUBLANES = 8
V7X_VMEM_BYTES = 64 * 1024 * 1024
VMEM_LIMIT_BYTES = V7X_VMEM_BYTES - 6 * 1024 * 1024

QK_PAD = 2 * V7X_LANES
KV_WIDTH = NOPE_DIM + V_DIM
ATTN_HEADS_PER_STEP = 4
SGU_GROUPS_PER_STEP = 2

F32 = jnp.float32
BF16 = jnp.bfloat16


def _params(n_axes):
    return pltpu.CompilerParams(
        dimension_semantics=("arbitrary",) * n_axes,
        vmem_limit_bytes=VMEM_LIMIT_BYTES,
    )


def _gelu_tanh(x):
    c = (2.0 / jnp.pi) ** 0.5
    return 0.5 * x * (1.0 + jnp.tanh(c * (x + 0.044715 * (x * x * x))))


def _layer_norm(y, g, b):
    mu = jnp.mean(y, axis=-1, keepdims=True)
    d = y - mu
    var = jnp.mean(d * d, axis=-1, keepdims=True)
    return d * lax.rsqrt(var + LN_EPS) * g + b


def _rms_norm(y, g):
    return y * lax.rsqrt(jnp.mean(y * y, axis=-1, keepdims=True) + RMS_EPS) * g


def _store_residual_ln(x_ref, m, g_ref, b_ref, xo_ref, xob_ref, alpha):
    y = _layer_norm(alpha * x_ref[...] + m, g_ref[...], b_ref[...])
    xo_ref[...] = y
    xob_ref[...] = y.astype(BF16)


def _cast_kernel(w_ref, o_ref, *, segments, n_in_blocks, n_out_blocks):
    def copy():
        for dst, src, width in segments:
            if src is None:
                o_ref[:, dst:dst + width] = jnp.zeros((o_ref.shape[0], width), o_ref.dtype)
            else:
                o_ref[:, dst:dst + width] = w_ref[:, src:src + width].astype(o_ref.dtype)

    if n_in_blocks == n_out_blocks:
        copy()
    else:
        r = pl.program_id(1)
        pl.when(r < n_in_blocks)(copy)

        @pl.when(r >= n_in_blocks)
        def _():
            o_ref[...] = jnp.zeros_like(o_ref)


def _cast_weights(w, tr, segments=None, n_out=None, k_out=None):
    n_layers, k, n = w.shape
    n_out = n if n_out is None else n_out
    k_out = k if k_out is None else k_out
    segments = ((0, 0, n),) if segments is None else tuple(segments)
    assert k % tr == 0 and k_out % tr == 0
    n_in_blocks, n_out_blocks = k // tr, k_out // tr
    kern = functools.partial(_cast_kernel, segments=segments, n_in_blocks=n_in_blocks,
                             n_out_blocks=n_out_blocks)
    return pl.pallas_call(
        kern,
        grid=(n_layers, n_out_blocks),
        in_specs=[pl.BlockSpec((None, tr, n), lambda l, r: (l, jnp.minimum(r, n_in_blocks - 1), 0))],
        out_specs=pl.BlockSpec((None, tr, n_out), lambda l, r: (l, r, 0)),
        out_shape=jax.ShapeDtypeStruct((n_layers, k_out, n_out), BF16),
        compiler_params=_params(2),
        name="cast_weights",
    )(w)


def _gmlp_in_kernel(x_ref, w_ref, z_ref):
    h = jnp.dot(x_ref[...], w_ref[...], preferred_element_type=F32)
    z_ref[...] = _gelu_tanh(h).astype(z_ref.dtype)


def _gmlp_in(xb, w_in, slot, tm, tn):
    t, d = xb.shape
    n = w_in.shape[2]
    return pl.pallas_call(
        _gmlp_in_kernel,
        grid=(t // tm, n // tn),
        in_specs=[
            pl.BlockSpec((tm, d), lambda i, j: (i, 0)),
            pl.BlockSpec((None, d, tn), lambda i, j: (slot, 0, j)),
        ],
        out_specs=pl.BlockSpec((tm, tn), lambda i, j: (i, j)),
        out_shape=jax.ShapeDtypeStruct((t, n), BF16),
        compiler_params=_params(2),
        name="gmlp_in",
    )(xb, w_in)


def _sgu_out_kernel(u_ref, v_ref, lng_ref, lnb_ref, ws_ref, bs_ref, wo_ref, x_ref, g_ref, b_ref,
                    xo_ref, xob_ref, p_ref, mu_ref, rs_ref, *, alpha, gd):
    grp = pl.program_id(1)
    acc_ref = xo_ref
    tm = u_ref.shape[0]

    @pl.when(grp == 0)
    def _():
        v = v_ref[...].astype(F32)
        mu = jnp.mean(v, axis=-1, keepdims=True)
        d = v - mu
        mu_ref[...] = mu
        rs_ref[...] = lax.rsqrt(jnp.mean(d * d, axis=-1, keepdims=True) + LN_EPS)
        acc_ref[...] = jnp.zeros_like(acc_ref)

    rows = lax.broadcasted_iota(jnp.int32, (GM_BLOCK, GM_BLOCK), 0) // CHUNK
    cols = lax.broadcasted_iota(jnp.int32, (GM_BLOCK, GM_BLOCK), 1) // CHUNK

    for gi in range(SGU_GROUPS_PER_STEP):
        off = pl.multiple_of((grp * SGU_GROUPS_PER_STEP + gi) * gd, V7X_LANES)
        cs = slice(gi * gd, (gi + 1) * gd)
        vg = v_ref[:, pl.ds(off, gd)].astype(F32)
        vn = ((vg - mu_ref[...]) * rs_ref[...] * lng_ref[:, cs] + lnb_ref[:, cs]).astype(BF16)
        wm = jnp.where(cols <= rows, ws_ref[gi], 0.0).astype(BF16)
        bs = bs_ref[gi]
        for r in range(tm // GM_BLOCK):
            sl = slice(r * GM_BLOCK, (r + 1) * GM_BLOCK)
            s = jnp.dot(wm, vn[sl, :], preferred_element_type=F32) + bs
            p_ref[sl, cs] = (u_ref[sl, cs].astype(F32) * s).astype(BF16)
    acc_ref[...] += jnp.dot(p_ref[...], wo_ref[...], preferred_element_type=F32)

    @pl.when(grp == pl.num_programs(1) - 1)
    def _():
        _store_residual_ln(x_ref, acc_ref[...], g_ref, b_ref, xo_ref, xob_ref, alpha)


def _sgu_out(z, ln_g, ln_b, w_s, b_s, w_out, x, g, b, slot, layer, alpha, tm):
    t, two_gh = z.shape
    gh = two_gh // 2
    gd = gh // GM_GROUPS
    d = x.shape[1]
    gps = SGU_GROUPS_PER_STEP
    kern = functools.partial(_sgu_out_kernel, alpha=alpha, gd=gd)
    return pl.pallas_call(
        kern,
        grid=(t // tm, GM_GROUPS // gps),
        in_specs=[
            pl.BlockSpec((tm, gps * gd), lambda i, k: (i, k)),
            pl.BlockSpec((tm, gh), lambda i, k: (i, 1)),
            pl.BlockSpec((None, 1, gps * gd), lambda i, k: (slot, 0, k)),
            pl.BlockSpec((None, 1, gps * gd), lambda i, k: (slot, 0, k)),
            pl.BlockSpec((None, gps, GM_BLOCK, GM_BLOCK), lambda i, k: (slot, k, 0, 0)),
            pl.BlockSpec((None, gps, GM_BLOCK, 1), lambda i, k: (slot, k, 0, 0)),
            pl.BlockSpec((None, gps * gd, d), lambda i, k: (slot, k, 0)),
            pl.BlockSpec((tm, d), lambda i, k: (i, 0)),
            pl.BlockSpec((None, 1, d), lambda i, k: (layer, 0, 0)),
            pl.BlockSpec((None, 1, d), lambda i, k: (layer, 0, 0)),
        ],
        out_specs=[
            pl.BlockSpec((tm, d), lambda i, k: (i, 0)),
            pl.BlockSpec((tm, d), lambda i, k: (i, 0)),
        ],
        out_shape=[jax.ShapeDtypeStruct((t, d), F32), jax.ShapeDtypeStruct((t, d), BF16)],
        scratch_shapes=[
            pltpu.VMEM((tm, gps * gd), BF16),
            pltpu.VMEM((tm, 1), F32),
            pltpu.VMEM((tm, 1), F32),
        ],
        compiler_params=_params(2),
        name="sgu_out",
    )(z, z, ln_g, ln_b, w_s, b_s, w_out, x, g, b)


def _ffn_kernel(xb_ref, wu_ref, cw_ref, cb_ref, wd_ref, x_ref, g_ref, b_ref,
                xo_ref, xob_ref, acc_ref, h0_ref, h1_ref, carry_ref, *, alpha, tiles_per_seq, n_ff):
    s = pl.program_id(0)

    @pl.when(s == 0)
    def _():
        h1_ref[...] = jnp.zeros_like(h1_ref)
        acc_ref[...] = jnp.zeros_like(acc_ref)
        carry_ref[...] = jnp.zeros_like(carry_ref)

    stage = functools.partial(
        _ffn_stage, xb_ref, wu_ref, cw_ref, cb_ref, wd_ref, x_ref, g_ref, b_ref, xo_ref, xob_ref,
        acc_ref, carry_ref, alpha=alpha, tiles_per_seq=tiles_per_seq, n_ff=n_ff)
    pl.when(s % 2 == 0)(functools.partial(stage, h1_ref, h0_ref))
    pl.when(s % 2 == 1)(functools.partial(stage, h0_ref, h1_ref))


def _ffn_stage(xb_ref, wu_ref, cw_ref, cb_ref, wd_ref, x_ref, g_ref, b_ref, xo_ref, xob_ref,
               acc_ref, carry_ref, h_rd_ref, h_wr_ref, *, alpha, tiles_per_seq, n_ff):
    s = pl.program_id(0)
    tm = xb_ref.shape[0]
    tn = wd_ref.shape[0]
    lag = jnp.maximum(s - 1, 0)
    ip = lag // n_ff
    jp = lag % n_ff

    h_wr_ref[...] = jnp.dot(xb_ref[...], wu_ref[...], preferred_element_type=F32)

    h = h_rd_ref[...]

    first = (ip % tiles_per_seq) == 0
    prev = jnp.where(first, 0.0, carry_ref[jp])
    carry_ref[jp] = h[tm - V7X_SUBLANES:, :]

    head_rows = lax.broadcasted_iota(jnp.int32, (V7X_SUBLANES, 2 * tn), 0)
    cw = cw_ref[...]
    out = cw[2:3, :] * h + cb_ref[...]
    for shift in (1, 2):
        hs = pltpu.roll(h, shift, 0)
        head = jnp.where(head_rows < shift, pltpu.roll(prev, shift, 0), hs[:V7X_SUBLANES, :])
        hs = jnp.concatenate([head, hs[V7X_SUBLANES:, :]], axis=0)
        out = out + cw[2 - shift:3 - shift, :] * hs

    a = out[:, :tn]
    gt = out[:, tn:]
    act = (gt * (1.0 / (1.0 + jnp.exp(-gt))) * a).astype(BF16)
    acc_ref[...] += jnp.dot(act, wd_ref[...], preferred_element_type=F32)

    @pl.when(s == 0)
    def _():
        acc_ref[...] = jnp.zeros_like(acc_ref)

    @pl.when((s > 0) & (jp == n_ff - 1))
    def _():
        _store_residual_ln(x_ref, acc_ref[...], g_ref, b_ref, xo_ref, xob_ref, alpha)
        acc_ref[...] = jnp.zeros_like(acc_ref)


def _conv_ffn(xb, w_up, conv_w, conv_b, w_down, x, g, b, layer, alpha, seq, tm, tn):
    t, d = x.shape
    n_ff = w_down.shape[1] // tn
    n_steps = (t // tm) * n_ff
    kern = functools.partial(_ffn_kernel, alpha=alpha, tiles_per_seq=seq // tm, n_ff=n_ff)

    def cur(s):
        c = jnp.minimum(s, n_steps - 1)
        return c // n_ff, c % n_ff

    def lag(s):
        c = jnp.maximum(s - 1, 0)
        return c // n_ff, c % n_ff

    return pl.pallas_call(
        kern,
        grid=(n_steps + 1,),
        in_specs=[
            pl.BlockSpec((tm, d), lambda s: (cur(s)[0], 0)),
            pl.BlockSpec((None, d, 2 * tn), lambda s: (layer, 0, cur(s)[1])),
            pl.BlockSpec((None, CONV_W, 2 * tn), lambda s: (layer, 0, lag(s)[1])),
            pl.BlockSpec((None, 1, 2 * tn), lambda s: (layer, 0, lag(s)[1])),
            pl.BlockSpec((None, tn, d), lambda s: (layer, lag(s)[1], 0)),
            pl.BlockSpec((tm, d), lambda s: (lag(s)[0], 0)),
            pl.BlockSpec((None, 1, d), lambda s: (layer, 0, 0)),
            pl.BlockSpec((None, 1, d), lambda s: (layer, 0, 0)),
        ],
        out_specs=[
            pl.BlockSpec((tm, d), lambda s: (lag(s)[0], 0)),
            pl.BlockSpec((tm, d), lambda s: (lag(s)[0], 0)),
        ],
        out_shape=[jax.ShapeDtypeStruct((t, d), F32), jax.ShapeDtypeStruct((t, d), BF16)],
        scratch_shapes=[
            pltpu.VMEM((tm, d), F32),
            pltpu.VMEM((tm, 2 * tn), F32),
            pltpu.VMEM((tm, 2 * tn), F32),
            pltpu.VMEM((n_ff, V7X_SUBLANES, 2 * tn), F32),
        ],
        compiler_params=_params(1),
        name="conv_ffn",
    )(xb, w_up, conv_w, conv_b, w_down, x, g, b)


def _rope_pad(x, t0, t1, t2):
    half = ROPE_DIM // 2
    return x * t0 + pltpu.roll(x, half, 1) * t1 + pltpu.roll(x, V7X_LANES - half, 1) * t2


def _mla_proj_kernel(xb_ref, win_ref, qg_ref, kvg_ref, wq_ref, wkv_ref, t0_ref, t1_ref, t2_ref,
                     q_ref, kc_ref, v_ref):
    h = jnp.dot(xb_ref[...], win_ref[...], preferred_element_type=F32)
    cq = _rms_norm(h[:, :Q_RANK], qg_ref[...]).astype(BF16)
    ckv = _rms_norm(h[:, Q_RANK:Q_RANK + KV_RANK], kvg_ref[...]).astype(BF16)
    t0, t1, t2 = t0_ref[...], t1_ref[...], t2_ref[...]
    kr = _rope_pad(h[:, Q_RANK + KV_RANK:], t0, t1, t2).astype(BF16)

    q = jnp.dot(cq, wq_ref[...], preferred_element_type=F32)
    kv = jnp.dot(ckv, wkv_ref[...], preferred_element_type=F32)
    for hd in range(MLA_HEADS):
        qb = hd * QK_PAD
        kb = hd * KV_WIDTH
        q_ref[:, qb:qb + NOPE_DIM] = q[:, qb:qb + NOPE_DIM].astype(BF16)
        q_ref[:, qb + NOPE_DIM:qb + QK_PAD] = _rope_pad(
            q[:, qb + NOPE_DIM:qb + QK_PAD], t0, t1, t2).astype(BF16)
        kc_ref[:, qb:qb + NOPE_DIM] = kv[:, kb:kb + NOPE_DIM].astype(BF16)
        kc_ref[:, qb + NOPE_DIM:qb + QK_PAD] = kr
        v_ref[:, hd * V_DIM:(hd + 1) * V_DIM] = kv[:, kb + NOPE_DIM:kb + KV_WIDTH].astype(BF16)


def _mla_proj(xb, w_in, q_g, kv_g, w_q, w_kv, t0, t1, t2, slot, seq, tm):
    t, d = xb.shape
    n_in = w_in.shape[1]
    tiles_per_seq = seq // tm
    hq = MLA_HEADS * QK_PAD
    rope_spec = pl.BlockSpec((tm, V7X_LANES), lambda i: (i % tiles_per_seq, 0))
    return pl.pallas_call(
        _mla_proj_kernel,
        grid=(t // tm,),
        in_specs=[
            pl.BlockSpec((tm, d), lambda i: (i, 0)),
            pl.BlockSpec((d, n_in), lambda i: (0, 0)),
            pl.BlockSpec((None, 1, Q_RANK), lambda i: (slot, 0, 0)),
            pl.BlockSpec((None, 1, KV_RANK), lambda i: (slot, 0, 0)),
            pl.BlockSpec((Q_RANK, hq), lambda i: (0, 0)),
            pl.BlockSpec((None, KV_RANK, MLA_HEADS * KV_WIDTH), lambda i: (slot, 0, 0)),
            rope_spec, rope_spec, rope_spec,
        ],
        out_specs=[
            pl.BlockSpec((tm, hq), lambda i: (i, 0)),
            pl.BlockSpec((tm, hq), lambda i: (i, 0)),
            pl.BlockSpec((tm, MLA_HEADS * V_DIM), lambda i: (i, 0)),
        ],
        out_shape=[
            jax.ShapeDtypeStruct((t, hq), BF16),
            jax.ShapeDtypeStruct((t, hq), BF16),
            jax.ShapeDtypeStruct((t, MLA_HEADS * V_DIM), BF16),
        ],
        compiler_params=_params(1),
        name="mla_proj",
    )(xb, w_in, q_g, kv_g, w_q, w_kv, t0, t1, t2)


def _attn_kernel(q_ref, k_ref, v_ref, o_ref, *, tk):
    qi = pl.program_id(2)
    tq = q_ref.shape[0]
    c = SM_SCALE * LOG2_E
    qs = [q_ref[:, hd * QK_PAD:(hd + 1) * QK_PAD] for hd in range(ATTN_HEADS_PER_STEP)]

    def step(kb, carry, diagonal):
        off = pl.multiple_of(kb * tk, tk)
        new = []
        for hd in range(ATTN_HEADS_PER_STEP):
            m, l, acc = carry[hd]
            k = k_ref[pl.ds(off, tk), hd * QK_PAD:(hd + 1) * QK_PAD]
            v = v_ref[pl.ds(off, tk), hd * V_DIM:(hd + 1) * V_DIM]
            s = lax.dot_general(qs[hd], k, (((1,), (1,)), ((), ())), preferred_element_type=F32)
            if diagonal:
                rows = lax.broadcasted_iota(jnp.int32, (tq, tk), 0) // CHUNK
                cols = lax.broadcasted_iota(jnp.int32, (tq, tk), 1) // CHUNK
                s = jnp.where(cols <= rows, s, -jnp.inf)
            m_new = jnp.maximum(m, jnp.max(s, axis=-1, keepdims=True))
            a = jnp.exp2((m - m_new) * c)
            p = jnp.exp2((s - m_new) * c)
            l = a * l + jnp.sum(p, axis=-1, keepdims=True)
            acc = a * acc + jnp.dot(p.astype(BF16), v, preferred_element_type=F32)
            new.append((m_new, l, acc))
        return tuple(new)

    init = tuple(
        (jnp.full((tq, 1), -jnp.inf, F32), jnp.zeros((tq, 1), F32), jnp.zeros((tq, V_DIM), F32))
        for _ in range(ATTN_HEADS_PER_STEP))
    carry = lax.fori_loop(0, qi, lambda kb, cr: step(kb, cr, False), init)
    carry = step(qi, carry, True)
    for hd in range(ATTN_HEADS_PER_STEP):
        _, l, acc = carry[hd]
        o_ref[:, hd * V_DIM:(hd + 1) * V_DIM] = (acc / l).astype(o_ref.dtype)


def _attention(q, kc, v, batch, seq, tq):
    t = q.shape[0]
    nq = seq // tq
    hps = ATTN_HEADS_PER_STEP
    kern = functools.partial(_attn_kernel, tk=tq)
    return pl.pallas_call(
        kern,
        grid=(batch, MLA_HEADS // hps, nq),
        in_specs=[
            pl.BlockSpec((tq, hps * QK_PAD), lambda b, h, i: (b * nq + i, h)),
            pl.BlockSpec((seq, hps * QK_PAD), lambda b, h, i: (b, h)),
            pl.BlockSpec((seq, hps * V_DIM), lambda b, h, i: (b, h)),
        ],
        out_specs=pl.BlockSpec((tq, hps * V_DIM), lambda b, h, i: (b * nq + i, h)),
        out_shape=jax.ShapeDtypeStruct((t, MLA_HEADS * V_DIM), BF16),
        compiler_params=_params(3),
        name="attention",
    )(q, kc, v)


def _attn_out_kernel(o_ref, w_ref, x_ref, g_ref, b_ref, xo_ref, xob_ref, *, alpha):
    m = jnp.dot(o_ref[...], w_ref[...], preferred_element_type=F32)
    _store_residual_ln(x_ref, m, g_ref, b_ref, xo_ref, xob_ref, alpha)


def _attn_out(o, w_out, x, g, b, slot, layer, alpha, tm):
    t, d = x.shape
    k = o.shape[1]
    kern = functools.partial(_attn_out_kernel, alpha=alpha)
    return pl.pallas_call(
        kern,
        grid=(t // tm,),
        in_specs=[
            pl.BlockSpec((tm, k), lambda i: (i, 0)),
            pl.BlockSpec((None, k, d), lambda i: (slot, 0, 0)),
            pl.BlockSpec((tm, d), lambda i: (i, 0)),
            pl.BlockSpec((None, 1, d), lambda i: (layer, 0, 0)),
            pl.BlockSpec((None, 1, d), lambda i: (layer, 0, 0)),
        ],
        out_specs=[
            pl.BlockSpec((tm, d), lambda i: (i, 0)),
            pl.BlockSpec((tm, d), lambda i: (i, 0)),
        ],
        out_shape=[jax.ShapeDtypeStruct((t, d), F32), jax.ShapeDtypeStruct((t, d), BF16)],
        compiler_params=_params(1),
        name="attn_out",
    )(o, w_out, x, g, b)


def _round_up(n, m):
    return (n + m - 1) // m * m


def _ffn_segments(d_ff, tn):
    segs = []
    for j in range(_round_up(d_ff, tn) // tn):
        width = min(tn, d_ff - j * tn)
        for half in range(2):
            dst = (2 * j + half) * tn
            segs.append((dst, half * d_ff + j * tn, width))
            if width < tn:
                segs.append((dst + width, None, tn - width))
    return segs


def _interleave_halves(w, d_ff, tn):
    parts = []
    for dst, src, width in _ffn_segments(d_ff, tn):
        parts.append(jnp.zeros(w.shape[:-1] + (width,), w.dtype) if src is None
                     else w[..., src:src + width])
    return jnp.concatenate(parts, axis=-1)


def _rope_tables(seq):
    half = ROPE_DIM // 2
    inv_freq = ROPE_THETA ** (-jnp.arange(half, dtype=F32) / half)
    ang = jnp.arange(seq, dtype=F32)[:, None] * inv_freq[None, :]
    cos, sin = jnp.cos(ang), jnp.sin(ang)
    z = jnp.zeros_like(cos)
    zz = jnp.zeros((seq, V7X_LANES - ROPE_DIM), F32)
    t0 = jnp.concatenate([cos, cos, zz], axis=1)
    t1 = jnp.concatenate([z, sin, zz], axis=1)
    t2 = jnp.concatenate([-sin, z, zz], axis=1)
    return t0, t1, t2


def _pad_heads(w, width):
    k = w.shape[0]
    w = w.reshape(k, MLA_HEADS, width)
    w = jnp.pad(w, ((0, 0), (0, 0), (0, QK_PAD - width)))
    return w.reshape(k, MLA_HEADS * QK_PAD)


def kernel(x, gm_w_in, gm_ln_g, gm_ln_b, gm_w_s, gm_b_s, gm_w_out, mla_w_in, mla_q_norm_g, mla_kv_norm_g, mla_w_q_b, mla_w_kv_b, mla_w_out, ffn_w_up, ffn_conv_w, ffn_conv_b, ffn_w_down, ln_mix_g, ln_mix_b, ln_ffn_g, ln_ffn_b):
    batch, seq, d = x.shape
    t = batch * seq
    depth = ffn_w_up.shape[0]
    d_ff = ffn_w_down.shape[1]
    alpha = float((2 * depth) ** 0.25)

    tm = min(512, seq)
    tm_in = min(1024, seq)
    tn_in = 2048
    tn_ff = 512
    tq = min(512, seq)
    assert seq % tm == 0 and seq % tm_in == 0 and seq % tq == 0 and tm % GM_BLOCK == 0
    assert (2 * gm_w_out.shape[1]) % tn_in == 0 and tq % CHUNK == 0
    ffp = _round_up(d_ff, tn_ff)

    gm_w_in_b = _cast_weights(gm_w_in, tr=256)
    gm_w_out_b = _cast_weights(gm_w_out, tr=1024)
    mla_w_kv_b = _cast_weights(mla_w_kv_b, tr=KV_RANK)
    mla_w_out_b = _cast_weights(mla_w_out, tr=1024)
    ffn_w_up_b = _cast_weights(ffn_w_up, tr=256, segments=_ffn_segments(d_ff, tn_ff), n_out=2 * ffp)
    ffn_w_down_b = _cast_weights(ffn_w_down, tr=V7X_LANES, k_out=ffp)
    conv_w = _interleave_halves(ffn_conv_w, d_ff, tn_ff)
    conv_b = _interleave_halves(ffn_conv_b[:, None, :], d_ff, tn_ff)

    gm_ln_g3, gm_ln_b3 = gm_ln_g[:, None, :], gm_ln_b[:, None, :]
    gm_b_s4 = gm_b_s[:, :, :, None]
    q_g3, kv_g3 = mla_q_norm_g[:, None, :], mla_kv_norm_g[:, None, :]
    mix_g3, mix_b3 = ln_mix_g[:, None, :], ln_mix_b[:, None, :]
    ffn_g3, ffn_b3 = ln_ffn_g[:, None, :], ln_ffn_b[:, None, :]

    xf = x.reshape(t, d)
    xb = xf.astype(BF16)
    t0, t1, t2 = _rope_tables(seq)

    for i in range(depth):
        slot = i // 2
        if i % 2 == 0:
            z = _gmlp_in(xb, gm_w_in_b, slot, tm_in, tn_in)
            xf, xb = _sgu_out(z, gm_ln_g3, gm_ln_b3, gm_w_s, gm_b_s4, gm_w_out_b, xf,
                              mix_g3, mix_b3, slot, i, alpha, tm)
        else:
            w_in = jnp.pad(mla_w_in[slot], ((0, 0), (0, V7X_LANES - ROPE_DIM))).astype(BF16)
            w_q = _pad_heads(mla_w_q_b[slot], NOPE_DIM + ROPE_DIM).astype(BF16)
            q, kc, v = _mla_proj(xb, w_in, q_g3, kv_g3, w_q, mla_w_kv_b, t0, t1, t2, slot, seq, tm)
            o = _attention(q, kc, v, batch, seq, tq)
            xf, xb = _attn_out(o, mla_w_out_b, xf, mix_g3, mix_b3, slot, i, alpha, tm)

        xf, xb = _conv_ffn(xb, ffn_w_up_b, conv_w, conv_b, ffn_w_down_b, xf, ffn_g3, ffn_b3,
                           i, alpha, seq, tm, tn_ff)

    return xf.reshape(batch, seq, d)
```

```python
import functools

import jax
import jax.numpy as jnp
from jax import lax
from jax.experimental import pallas as pl
from jax.experimental.pallas import tpu as pltpu

CHUNK = 64
GM_BLOCK = 128
GM_GROUPS = 8
MLA_HEADS = 16
Q_RANK = 512
KV_RANK = 512
NOPE_DIM = 128
ROPE_DIM = 64
V_DIM = 128
ROPE_THETA = 10000.0
SM_SCALE = (NOPE_DIM + ROPE_DIM) ** -0.5
LOG2_E = 1.4426950408889634
CONV_W = 3
LN_EPS = 1e-5
RMS_EPS = 1e-6

V7X_LANES = 128
V7X_SUBLANES = 8
V7X_VMEM_BYTES = 64 * 1024 * 1024
VMEM_LIMIT_BYTES = V7X_VMEM_BYTES - 6 * 1024 * 1024

QK_PAD = 2 * V7X_LANES
KV_WIDTH = NOPE_DIM + V_DIM
ATTN_HEADS_PER_STEP = 4
SGU_GROUPS_PER_STEP = 2

F32 = jnp.float32
BF16 = jnp.bfloat16


def _params(n_axes):
    return pltpu.CompilerParams(
        dimension_semantics=("arbitrary",) * n_axes,
        vmem_limit_bytes=VMEM_LIMIT_BYTES,
    )


def _gelu_tanh(x):
    c = (2.0 / jnp.pi) ** 0.5
    half_x = 0.5 * x
    return half_x + half_x * jnp.tanh(x * (c + (c * 0.044715) * (x * x)))


def _layer_norm(y, g, b):
    mu = jnp.mean(y, axis=-1, keepdims=True)
    d = y - mu
    var = jnp.mean(d * d, axis=-1, keepdims=True)
    return d * lax.rsqrt(var + LN_EPS) * g + b


def _rms_norm(y, g):
    return y * lax.rsqrt(jnp.mean(y * y, axis=-1, keepdims=True) + RMS_EPS) * g


def _store_residual_ln(x_ref, m, g_ref, b_ref, xo_ref, xob_ref, alpha):
    y = _layer_norm(alpha * x_ref[...] + m, g_ref[...], b_ref[...])
    xo_ref[...] = y
    xob_ref[...] = y.astype(BF16)


def _cast_kernel(w_ref, o_ref, *, segments, n_in_blocks, n_out_blocks):
    def copy():
        for dst, src, width in segments:
            if src is None:
                o_ref[:, dst:dst + width] = jnp.zeros((o_ref.shape[0], width), o_ref.dtype)
            else:
                o_ref[:, dst:dst + width] = w_ref[:, src:src + width].astype(o_ref.dtype)

    if n_in_blocks == n_out_blocks:
        copy()
    else:
        r = pl.program_id(1)
        pl.when(r < n_in_blocks)(copy)

        @pl.when(r >= n_in_blocks)
        def _():
            o_ref[...] = jnp.zeros_like(o_ref)


def _cast_weights(w, tr, segments=None, n_out=None, k_out=None):
    n_layers, k, n = w.shape
    n_out = n if n_out is None else n_out
    k_out = k if k_out is None else k_out
    segments = ((0, 0, n),) if segments is None else tuple(segments)
    assert k % tr == 0 and k_out % tr == 0
    n_in_blocks, n_out_blocks = k // tr, k_out // tr
    kern = functools.partial(_cast_kernel, segments=segments, n_in_blocks=n_in_blocks,
                             n_out_blocks=n_out_blocks)
    return pl.pallas_call(
        kern,
        grid=(n_layers, n_out_blocks),
        in_specs=[pl.BlockSpec((None, tr, n), lambda l, r: (l, jnp.minimum(r, n_in_blocks - 1), 0))],
        out_specs=pl.BlockSpec((None, tr, n_out), lambda l, r: (l, r, 0)),
        out_shape=jax.ShapeDtypeStruct((n_layers, k_out, n_out), BF16),
        compiler_params=_params(2),
        name="cast_weights",
    )(w)


def _gmlp_in_kernel(x_ref, w_ref, z_ref):
    h = jnp.dot(x_ref[...].astype(BF16), w_ref[...], preferred_element_type=F32)
    z_ref[...] = _gelu_tanh(h).astype(z_ref.dtype)


def _gmlp_in(xb, w_in, slot, tm, tn):
    t, d = xb.shape
    n = w_in.shape[2]
    return pl.pallas_call(
        _gmlp_in_kernel,
        grid=(t // tm, n // tn),
        in_specs=[
            pl.BlockSpec((tm, d), lambda i, j: (i, 0)),
            pl.BlockSpec((None, d, tn), lambda i, j: (slot, 0, j)),
        ],
        out_specs=pl.BlockSpec((tm, tn), lambda i, j: (i, j)),
        out_shape=jax.ShapeDtypeStruct((t, n), BF16),
        compiler_params=_params(2),
        name="gmlp_in",
    )(xb, w_in)


def _sgu_out_kernel(u_ref, v_ref, lng_ref, lnb_ref, ws_ref, bs_ref, wo_ref, x_ref, g_ref, b_ref,
                    xo_ref, xob_ref, p_ref, mu_ref, rs_ref, *, alpha, gd):
    grp = pl.program_id(1)
    acc_ref = xo_ref
    tm = u_ref.shape[0]

    @pl.when(grp == 0)
    def _():
        v = v_ref[...].astype(F32)
        mu = jnp.mean(v, axis=-1, keepdims=True)
        d = v - mu
        mu_ref[...] = mu
        rs_ref[...] = lax.rsqrt(jnp.mean(d * d, axis=-1, keepdims=True) + LN_EPS)
        acc_ref[...] = jnp.zeros_like(acc_ref)

    rows = lax.broadcasted_iota(jnp.int32, (GM_BLOCK, GM_BLOCK), 0) // CHUNK
    cols = lax.broadcasted_iota(jnp.int32, (GM_BLOCK, GM_BLOCK), 1) // CHUNK

    for gi in range(SGU_GROUPS_PER_STEP):
        off = pl.multiple_of((grp * SGU_GROUPS_PER_STEP + gi) * gd, V7X_LANES)
        cs = slice(gi * gd, (gi + 1) * gd)
        vg = v_ref[:, pl.ds(off, gd)].astype(F32)
        vn = ((vg - mu_ref[...]) * rs_ref[...] * lng_ref[:, cs] + lnb_ref[:, cs]).astype(BF16)
        wm = jnp.where(cols <= rows, ws_ref[gi], 0.0).astype(BF16)
        bs = bs_ref[gi]
        for r in range(tm // GM_BLOCK):
            sl = slice(r * GM_BLOCK, (r + 1) * GM_BLOCK)
            s = jnp.dot(wm, vn[sl, :], preferred_element_type=F32) + bs
            p_ref[sl, cs] = (u_ref[sl, cs].astype(F32) * s).astype(BF16)
    acc_ref[...] += jnp.dot(p_ref[...], wo_ref[...], preferred_element_type=F32)

    @pl.when(grp == pl.num_programs(1) - 1)
    def _():
        _store_residual_ln(x_ref, acc_ref[...], g_ref, b_ref, xo_ref, xob_ref, alpha)


def _sgu_out(z, ln_g, ln_b, w_s, b_s, w_out, x, g, b, slot, layer, alpha, tm):
    t, two_gh = z.shape
    gh = two_gh // 2
    gd = gh // GM_GROUPS
    d = x.shape[1]
    gps = SGU_GROUPS_PER_STEP
    kern = functools.partial(_sgu_out_kernel, alpha=alpha, gd=gd)
    return pl.pallas_call(
        kern,
        grid=(t // tm, GM_GROUPS // gps),
        in_specs=[
            pl.BlockSpec((tm, gps * gd), lambda i, k: (i, k)),
            pl.BlockSpec((tm, gh), lambda i, k: (i, 1)),
            pl.BlockSpec((None, 1, gps * gd), lambda i, k: (slot, 0, k)),
            pl.BlockSpec((None, 1, gps * gd), lambda i, k: (slot, 0, k)),
            pl.BlockSpec((None, gps, GM_BLOCK, GM_BLOCK), lambda i, k: (slot, k, 0, 0)),
            pl.BlockSpec((None, gps, GM_BLOCK, 1), lambda i, k: (slot, k, 0, 0)),
            pl.BlockSpec((None, gps * gd, d), lambda i, k: (slot, k, 0)),
            pl.BlockSpec((tm, d), lambda i, k: (i, 0)),
            pl.BlockSpec((None, 1, d), lambda i, k: (layer, 0, 0)),
            pl.BlockSpec((None, 1, d), lambda i, k: (layer, 0, 0)),
        ],
        out_specs=[
            pl.BlockSpec((tm, d), lambda i, k: (i, 0)),
            pl.BlockSpec((tm, d), lambda i, k: (i, 0)),
        ],
        out_shape=[jax.ShapeDtypeStruct((t, d), F32), jax.ShapeDtypeStruct((t, d), BF16)],
        scratch_shapes=[
            pltpu.VMEM((tm, gps * gd), BF16),
            pltpu.VMEM((tm, 1), F32),
            pltpu.VMEM((tm, 1), F32),
        ],
        compiler_params=_params(2),
        name="sgu_out",
    )(z, z, ln_g, ln_b, w_s, b_s, w_out, x, g, b)


def _ffn_kernel(xb_ref, wu_ref, cw_ref, cb_ref, wd_ref, x_ref, g_ref, b_ref,
                xo_ref, xob_ref, acc_ref, h0_ref, h1_ref, carry_ref, *, alpha, tiles_per_seq, n_ff):
    s = pl.program_id(0)

    @pl.when(s == 0)
    def _():
        h1_ref[...] = jnp.zeros_like(h1_ref)
        acc_ref[...] = jnp.zeros_like(acc_ref)
        carry_ref[...] = jnp.zeros_like(carry_ref)

    stage = functools.partial(
        _ffn_stage, xb_ref, wu_ref, cw_ref, cb_ref, wd_ref, x_ref, g_ref, b_ref, xo_ref, xob_ref,
        acc_ref, carry_ref, alpha=alpha, tiles_per_seq=tiles_per_seq, n_ff=n_ff)
    pl.when(s % 2 == 0)(functools.partial(stage, h1_ref, h0_ref))
    pl.when(s % 2 == 1)(functools.partial(stage, h0_ref, h1_ref))


def _ffn_stage(xb_ref, wu_ref, cw_ref, cb_ref, wd_ref, x_ref, g_ref, b_ref, xo_ref, xob_ref,
               acc_ref, carry_ref, h_rd_ref, h_wr_ref, *, alpha, tiles_per_seq, n_ff):
    s = pl.program_id(0)
    tm = xb_ref.shape[0]
    tn = wd_ref.shape[0]
    lag = jnp.maximum(s - 1, 0)
    ip = lag // n_ff
    jp = lag % n_ff

    h_wr_ref[...] = jnp.dot(xb_ref[...], wu_ref[...], preferred_element_type=F32)

    h = h_rd_ref[...]

    first = (ip % tiles_per_seq) == 0
    prev = jnp.where(first, 0.0, carry_ref[jp])
    carry_ref[jp] = h[tm - V7X_SUBLANES:, :]

    head_rows = lax.broadcasted_iota(jnp.int32, (V7X_SUBLANES, 2 * tn), 0)
    cw = cw_ref[...]
    out = cw[2:3, :] * h + cb_ref[...]
    for shift in (1, 2):
        hs = pltpu.roll(h, shift, 0)
        head = jnp.where(head_rows < shift, pltpu.roll(prev, shift, 0), hs[:V7X_SUBLANES, :])
        hs = jnp.concatenate([head, hs[V7X_SUBLANES:, :]], axis=0)
        out = out + cw[2 - shift:3 - shift, :] * hs

    a = out[:, :tn]
    gt = out[:, tn:]
    act = (gt * (1.0 / (1.0 + jnp.exp2(gt * (-LOG2_E)))) * a).astype(BF16)
    acc_ref[...] += jnp.dot(act, wd_ref[...], preferred_element_type=F32)

    @pl.when(s == 0)
    def _():
        acc_ref[...] = jnp.zeros_like(acc_ref)

    @pl.when((s > 0) & (jp == n_ff - 1))
    def _():
        _store_residual_ln(x_ref, acc_ref[...], g_ref, b_ref, xo_ref, xob_ref, alpha)
        acc_ref[...] = jnp.zeros_like(acc_ref)


def _conv_ffn(xb, w_up, conv_w, conv_b, w_down, x, g, b, layer, alpha, seq, tm, tn):
    t, d = x.shape
    n_ff = w_down.shape[1] // tn
    n_steps = (t // tm) * n_ff
    kern = functools.partial(_ffn_kernel, alpha=alpha, tiles_per_seq=seq // tm, n_ff=n_ff)

    def cur(s):
        c = jnp.minimum(s, n_steps - 1)
        return c // n_ff, c % n_ff

    def lag(s):
        c = jnp.maximum(s - 1, 0)
        return c // n_ff, c % n_ff

    return pl.pallas_call(
        kern,
        grid=(n_steps + 1,),
        in_specs=[
            pl.BlockSpec((tm, d), lambda s: (cur(s)[0], 0)),
            pl.BlockSpec((None, d, 2 * tn), lambda s: (layer, 0, cur(s)[1])),
            pl.BlockSpec((None, CONV_W, 2 * tn), lambda s: (layer, 0, lag(s)[1])),
            pl.BlockSpec((None, 1, 2 * tn), lambda s: (layer, 0, lag(s)[1])),
            pl.BlockSpec((None, tn, d), lambda s: (layer, lag(s)[1], 0)),
            pl.BlockSpec((tm, d), lambda s: (lag(s)[0], 0)),
            pl.BlockSpec((None, 1, d), lambda s: (layer, 0, 0)),
            pl.BlockSpec((None, 1, d), lambda s: (layer, 0, 0)),
        ],
        out_specs=[
            pl.BlockSpec((tm, d), lambda s: (lag(s)[0], 0)),
            pl.BlockSpec((tm, d), lambda s: (lag(s)[0], 0)),
        ],
        out_shape=[jax.ShapeDtypeStruct((t, d), F32), jax.ShapeDtypeStruct((t, d), BF16)],
        scratch_shapes=[
            pltpu.VMEM((tm, d), F32),
            pltpu.VMEM((tm, 2 * tn), F32),
            pltpu.VMEM((tm, 2 * tn), F32),
            pltpu.VMEM((n_ff, V7X_SUBLANES, 2 * tn), F32),
        ],
        compiler_params=_params(1),
        name="conv_ffn",
    )(xb, w_up, conv_w, conv_b, w_down, x, g, b)


def _rope_pad(x, t0, t1, t2):
    half = ROPE_DIM // 2
    return x * t0 + pltpu.roll(x, half, 1) * t1 + pltpu.roll(x, V7X_LANES - half, 1) * t2


def _mla_proj_kernel(xb_ref, win_ref, qg_ref, kvg_ref, wq_ref, wkv_ref, t0_ref, t1_ref, t2_ref,
                     q_ref, kc_ref, v_ref):
    h = jnp.dot(xb_ref[...], win_ref[...], preferred_element_type=F32)
    cq = _rms_norm(h[:, :Q_RANK], qg_ref[...]).astype(BF16)
    ckv = _rms_norm(h[:, Q_RANK:Q_RANK + KV_RANK], kvg_ref[...]).astype(BF16)
    t0, t1, t2 = t0_ref[...], t1_ref[...], t2_ref[...]
    kr = _rope_pad(h[:, Q_RANK + KV_RANK:], t0, t1, t2).astype(BF16)

    q = jnp.dot(cq, wq_ref[...], preferred_element_type=F32)
    kv = jnp.dot(ckv, wkv_ref[...], preferred_element_type=F32)
    for hd in range(MLA_HEADS):
        qb = hd * QK_PAD
        kb = hd * KV_WIDTH
        q_ref[:, qb:qb + NOPE_DIM] = q[:, qb:qb + NOPE_DIM].astype(BF16)
        q_ref[:, qb + NOPE_DIM:qb + QK_PAD] = _rope_pad(
            q[:, qb + NOPE_DIM:qb + QK_PAD], t0, t1, t2).astype(BF16)
        kc_ref[:, qb:qb + NOPE_DIM] = kv[:, kb:kb + NOPE_DIM].astype(BF16)
        kc_ref[:, qb + NOPE_DIM:qb + QK_PAD] = kr
        v_ref[:, hd * V_DIM:(hd + 1) * V_DIM] = kv[:, kb + NOPE_DIM:kb + KV_WIDTH].astype(BF16)


def _mla_proj(xb, w_in, q_g, kv_g, w_q, w_kv, t0, t1, t2, slot, seq, tm):
    t, d = xb.shape
    n_in = w_in.shape[1]
    tiles_per_seq = seq // tm
    hq = MLA_HEADS * QK_PAD
    rope_spec = pl.BlockSpec((tm, V7X_LANES), lambda i: (i % tiles_per_seq, 0))
    return pl.pallas_call(
        _mla_proj_kernel,
        grid=(t // tm,),
        in_specs=[
            pl.BlockSpec((tm, d), lambda i: (i, 0)),
            pl.BlockSpec((d, n_in), lambda i: (0, 0)),
            pl.BlockSpec((None, 1, Q_RANK), lambda i: (slot, 0, 0)),
            pl.BlockSpec((None, 1, KV_RANK), lambda i: (slot, 0, 0)),
            pl.BlockSpec((Q_RANK, hq), lambda i: (0, 0)),
            pl.BlockSpec((None, KV_RANK, MLA_HEADS * KV_WIDTH), lambda i: (slot, 0, 0)),
            rope_spec, rope_spec, rope_spec,
        ],
        out_specs=[
            pl.BlockSpec((tm, hq), lambda i: (i, 0)),
            pl.BlockSpec((tm, hq), lambda i: (i, 0)),
            pl.BlockSpec((tm, MLA_HEADS * V_DIM), lambda i: (i, 0)),
        ],
        out_shape=[
            jax.ShapeDtypeStruct((t, hq), BF16),
            jax.ShapeDtypeStruct((t, hq), BF16),
            jax.ShapeDtypeStruct((t, MLA_HEADS * V_DIM), BF16),
        ],
        compiler_params=_params(1),
        name="mla_proj",
    )(xb, w_in, q_g, kv_g, w_q, w_kv, t0, t1, t2)


def _attn_kernel(q_ref, k_ref, v_ref, o_ref, *, tk):
    qi = pl.program_id(2)
    tq = q_ref.shape[0]
    c = SM_SCALE * LOG2_E
    qs = [q_ref[:, hd * QK_PAD:(hd + 1) * QK_PAD] for hd in range(ATTN_HEADS_PER_STEP)]

    def step(kb, carry, diagonal):
        off = pl.multiple_of(kb * tk, tk)
        new = []
        for hd in range(ATTN_HEADS_PER_STEP):
            m, l, acc = carry[hd]
            k = k_ref[pl.ds(off, tk), hd * QK_PAD:(hd + 1) * QK_PAD]
            v = v_ref[pl.ds(off, tk), hd * V_DIM:(hd + 1) * V_DIM]
            s = lax.dot_general(qs[hd], k, (((1,), (1,)), ((), ())), preferred_element_type=F32)
            if diagonal:
                rows = lax.broadcasted_iota(jnp.int32, (tq, tk), 0) // CHUNK
                cols = lax.broadcasted_iota(jnp.int32, (tq, tk), 1) // CHUNK
                s = jnp.where(cols <= rows, s, -jnp.inf)
            m_new = jnp.maximum(m, jnp.max(s, axis=-1, keepdims=True))
            a = jnp.exp2((m - m_new) * c)
            p = jnp.exp2((s - m_new) * c)
            l = a * l + jnp.sum(p, axis=-1, keepdims=True)
            acc = a * acc + jnp.dot(p.astype(BF16), v, preferred_element_type=F32)
            new.append((m_new, l, acc))
        return tuple(new)

    init = tuple(
        (jnp.full((tq, 1), -jnp.inf, F32), jnp.zeros((tq, 1), F32), jnp.zeros((tq, V_DIM), F32))
        for _ in range(ATTN_HEADS_PER_STEP))
    carry = lax.fori_loop(0, qi, lambda kb, cr: step(kb, cr, False), init)
    carry = step(qi, carry, True)
    for hd in range(ATTN_HEADS_PER_STEP):
        _, l, acc = carry[hd]
        o_ref[:, hd * V_DIM:(hd + 1) * V_DIM] = (acc / l).astype(o_ref.dtype)


def _attention(q, kc, v, batch, seq, tq):
    t = q.shape[0]
    nq = seq // tq
    hps = ATTN_HEADS_PER_STEP
    kern = functools.partial(_attn_kernel, tk=tq)
    return pl.pallas_call(
        kern,
        grid=(batch, MLA_HEADS // hps, nq),
        in_specs=[
            pl.BlockSpec((tq, hps * QK_PAD), lambda b, h, i: (b * nq + i, h)),
            pl.BlockSpec((seq, hps * QK_PAD), lambda b, h, i: (b, h)),
            pl.BlockSpec((seq, hps * V_DIM), lambda b, h, i: (b, h)),
        ],
        out_specs=pl.BlockSpec((tq, hps * V_DIM), lambda b, h, i: (b * nq + i, h)),
        out_shape=jax.ShapeDtypeStruct((t, MLA_HEADS * V_DIM), BF16),
        compiler_params=_params(3),
        name="attention",
    )(q, kc, v)


def _attn_out_kernel(o_ref, w_ref, x_ref, g_ref, b_ref, xo_ref, xob_ref, *, alpha):
    m = jnp.dot(o_ref[...], w_ref[...], preferred_element_type=F32)
    _store_residual_ln(x_ref, m, g_ref, b_ref, xo_ref, xob_ref, alpha)


def _attn_out(o, w_out, x, g, b, slot, layer, alpha, tm):
    t, d = x.shape
    k = o.shape[1]
    kern = functools.partial(_attn_out_kernel, alpha=alpha)
    return pl.pallas_call(
        kern,
        grid=(t // tm,),
        in_specs=[
            pl.BlockSpec((tm, k), lambda i: (i, 0)),
            pl.BlockSpec((None, k, d), lambda i: (slot, 0, 0)),
            pl.BlockSpec((tm, d), lambda i: (i, 0)),
            pl.BlockSpec((None, 1, d), lambda i: (layer, 0, 0)),
            pl.BlockSpec((None, 1, d), lambda i: (layer, 0, 0)),
        ],
        out_specs=[
            pl.BlockSpec((tm, d), lambda i: (i, 0)),
            pl.BlockSpec((tm, d), lambda i: (i, 0)),
        ],
        out_shape=[jax.ShapeDtypeStruct((t, d), F32), jax.ShapeDtypeStruct((t, d), BF16)],
        compiler_params=_params(1),
        name="attn_out",
    )(o, w_out, x, g, b)


def _round_up(n, m):
    return (n + m - 1) // m * m


def _ffn_segments(d_ff, tn):
    segs = []
    for j in range(_round_up(d_ff, tn) // tn):
        width = min(tn, d_ff - j * tn)
        for half in range(2):
            dst = (2 * j + half) * tn
            segs.append((dst, half * d_ff + j * tn, width))
            if width < tn:
                segs.append((dst + width, None, tn - width))
    return segs


def _interleave_halves(w, d_ff, tn):
    parts = []
    for dst, src, width in _ffn_segments(d_ff, tn):
        parts.append(jnp.zeros(w.shape[:-1] + (width,), w.dtype) if src is None
                     else w[..., src:src + width])
    return jnp.concatenate(parts, axis=-1)


def _rope_tables(seq):
    half = ROPE_DIM // 2
    inv_freq = ROPE_THETA ** (-jnp.arange(half, dtype=F32) / half)
    ang = jnp.arange(seq, dtype=F32)[:, None] * inv_freq[None, :]
    cos, sin = jnp.cos(ang), jnp.sin(ang)
    z = jnp.zeros_like(cos)
    zz = jnp.zeros((seq, V7X_LANES - ROPE_DIM), F32)
    t0 = jnp.concatenate([cos, cos, zz], axis=1)
    t1 = jnp.concatenate([z, sin, zz], axis=1)
    t2 = jnp.concatenate([-sin, z, zz], axis=1)
    return t0, t1, t2


def _pad_heads(w, width):
    k = w.shape[0]
    w = w.reshape(k, MLA_HEADS, width)
    w = jnp.pad(w, ((0, 0), (0, 0), (0, QK_PAD - width)))
    return w.reshape(k, MLA_HEADS * QK_PAD)


def kernel(x, gm_w_in, gm_ln_g, gm_ln_b, gm_w_s, gm_b_s, gm_w_out, mla_w_in, mla_q_norm_g, mla_kv_norm_g, mla_w_q_b, mla_w_kv_b, mla_w_out, ffn_w_up, ffn_conv_w, ffn_conv_b, ffn_w_down, ln_mix_g, ln_mix_b, ln_ffn_g, ln_ffn_b):
    batch, seq, d = x.shape
    t = batch * seq
    depth = ffn_w_up.shape[0]
    d_ff = ffn_w_down.shape[1]
    alpha = float((2 * depth) ** 0.25)

    tm = min(512, seq)
    tm_in = min(1024, seq)
    tn_in = 2048
    tn_ff = 512
    tq = min(512, seq)
    assert seq % tm == 0 and seq % tm_in == 0 and seq % tq == 0 and tm % GM_BLOCK == 0
    assert (2 * gm_w_out.shape[1]) % tn_in == 0 and tq % CHUNK == 0
    ffp = _round_up(d_ff, tn_ff)

    gm_w_in_b = _cast_weights(gm_w_in, tr=256)
    gm_w_out_b = _cast_weights(gm_w_out, tr=1024)
    mla_w_kv_b = _cast_weights(mla_w_kv_b, tr=KV_RANK)
    mla_w_out_b = _cast_weights(mla_w_out, tr=1024)
    ffn_w_up_b = _cast_weights(ffn_w_up, tr=256, segments=_ffn_segments(d_ff, tn_ff), n_out=2 * ffp)
    ffn_w_down_b = _cast_weights(ffn_w_down, tr=V7X_LANES, k_out=ffp)
    conv_w = _interleave_halves(ffn_conv_w, d_ff, tn_ff)
    conv_b = _interleave_halves(ffn_conv_b[:, None, :], d_ff, tn_ff)

    gm_ln_g3, gm_ln_b3 = gm_ln_g[:, None, :], gm_ln_b[:, None, :]
    gm_b_s4 = gm_b_s[:, :, :, None]
    q_g3, kv_g3 = mla_q_norm_g[:, None, :], mla_kv_norm_g[:, None, :]
    mix_g3, mix_b3 = ln_mix_g[:, None, :], ln_mix_b[:, None, :]
    ffn_g3, ffn_b3 = ln_ffn_g[:, None, :], ln_ffn_b[:, None, :]

    xf = x.reshape(t, d)
    xb = xf
    t0, t1, t2 = _rope_tables(seq)

    for i in range(depth):
        slot = i // 2
        if i % 2 == 0:
            z = _gmlp_in(xb, gm_w_in_b, slot, tm_in, tn_in)
            xf, xb = _sgu_out(z, gm_ln_g3, gm_ln_b3, gm_w_s, gm_b_s4, gm_w_out_b, xf,
                              mix_g3, mix_b3, slot, i, alpha, tm)
        else:
            w_in = jnp.pad(mla_w_in[slot], ((0, 0), (0, V7X_LANES - ROPE_DIM))).astype(BF16)
            w_q = _pad_heads(mla_w_q_b[slot], NOPE_DIM + ROPE_DIM).astype(BF16)
            q, kc, v = _mla_proj(xb, w_in, q_g3, kv_g3, w_q, mla_w_kv_b, t0, t1, t2, slot, seq, tm)
            o = _attention(q, kc, v, batch, seq, tq)
            xf, xb = _attn_out(o, mla_w_out_b, xf, mix_g3, mix_b3, slot, i, alpha, tm)

        xf, xb = _conv_ffn(xb, ffn_w_up_b, conv_w, conv_b, ffn_w_down_b, xf, ffn_g3, ffn_b3,
                           i, alpha, seq, tm, tn_ff)

    return xf.reshape(batch, seq, d)
```

```python
import functools

import jax
import jax.numpy as jnp
from jax import lax
from jax.experimental import pallas as pl
from jax.experimental.pallas import tpu as pltpu

CHUNK = 64
GM_BLOCK = 128
GM_GROUPS = 8
MLA_HEADS = 16
Q_RANK = 512
KV_RANK = 512
NOPE_DIM = 128
ROPE_DIM = 64
V_DIM = 128
ROPE_THETA = 10000.0
SM_SCALE = (NOPE_DIM + ROPE_DIM) ** -0.5
LOG2_E = 1.4426950408889634
CONV_W = 3
LN_EPS = 1e-5
RMS_EPS = 1e-6

V7X_LANES = 128
V7X_SUBLANES = 8
V7X_VMEM_BYTES = 64 * 1024 * 1024
VMEM_LIMIT_BYTES = V7X_VMEM_BYTES - 6 * 1024 * 1024

QK_PAD = 2 * V7X_LANES
KV_WIDTH = NOPE_DIM + V_DIM
V_PAD = V_DIM + V7X_LANES
ATTN_HEADS_PER_STEP = 4
SGU_GROUPS_PER_STEP = 2

F32 = jnp.float32
BF16 = jnp.bfloat16


def _params(n_axes):
    return pltpu.CompilerParams(
        dimension_semantics=("arbitrary",) * n_axes,
        vmem_limit_bytes=VMEM_LIMIT_BYTES,
    )


def _gelu_tanh(x):
    c = (2.0 / jnp.pi) ** 0.5
    half_x = 0.5 * x
    return half_x + half_x * jnp.tanh(x * (c + (c * 0.044715) * (x * x)))


def _layer_norm(y, g, b):
    mu = jnp.mean(y, axis=-1, keepdims=True)
    d = y - mu
    var = jnp.mean(d * d, axis=-1, keepdims=True)
    return d * lax.rsqrt(var + LN_EPS) * g + b


def _rms_norm(y, g):
    return y * lax.rsqrt(jnp.mean(y * y, axis=-1, keepdims=True) + RMS_EPS) * g


def _store_residual_ln(x_ref, m, g_ref, b_ref, xo_ref, xob_ref, alpha):
    y = _layer_norm(alpha * x_ref[...] + m, g_ref[...], b_ref[...])
    xo_ref[...] = y
    xob_ref[...] = y.astype(BF16)


def _cast_kernel(w_ref, o_ref, *, segments, n_in_blocks, n_out_blocks):
    def copy():
        for dst, src, width in segments:
            if src is None:
                o_ref[:, dst:dst + width] = jnp.zeros((o_ref.shape[0], width), o_ref.dtype)
            else:
                o_ref[:, dst:dst + width] = w_ref[:, src:src + width].astype(o_ref.dtype)

    if n_in_blocks == n_out_blocks:
        copy()
    else:
        r = pl.program_id(1)
        pl.when(r < n_in_blocks)(copy)

        @pl.when(r >= n_in_blocks)
        def _():
            o_ref[...] = jnp.zeros_like(o_ref)


def _cast_weights(w, tr, segments=None, n_out=None, k_out=None):
    n_layers, k, n = w.shape
    n_out = n if n_out is None else n_out
    k_out = k if k_out is None else k_out
    segments = ((0, 0, n),) if segments is None else tuple(segments)
    assert k % tr == 0 and k_out % tr == 0
    n_in_blocks, n_out_blocks = k // tr, k_out // tr
    kern = functools.partial(_cast_kernel, segments=segments, n_in_blocks=n_in_blocks,
                             n_out_blocks=n_out_blocks)
    return pl.pallas_call(
        kern,
        grid=(n_layers, n_out_blocks),
        in_specs=[pl.BlockSpec((None, tr, n), lambda l, r: (l, jnp.minimum(r, n_in_blocks - 1), 0))],
        out_specs=pl.BlockSpec((None, tr, n_out), lambda l, r: (l, r, 0)),
        out_shape=jax.ShapeDtypeStruct((n_layers, k_out, n_out), BF16),
        compiler_params=_params(2),
        name="cast_weights",
    )(w)


def _gmlp_in_kernel(x_ref, w_ref, z_ref):
    h = jnp.dot(x_ref[...].astype(BF16), w_ref[...], preferred_element_type=F32)
    z_ref[...] = _gelu_tanh(h).astype(z_ref.dtype)


def _gmlp_in(xb, w_in, slot, tm, tn):
    t, d = xb.shape
    n = w_in.shape[2]
    return pl.pallas_call(
        _gmlp_in_kernel,
        grid=(t // tm, n // tn),
        in_specs=[
            pl.BlockSpec((tm, d), lambda i, j: (i, 0)),
            pl.BlockSpec((None, d, tn), lambda i, j: (slot, 0, j)),
        ],
        out_specs=pl.BlockSpec((tm, tn), lambda i, j: (i, j)),
        out_shape=jax.ShapeDtypeStruct((t, n), BF16),
        compiler_params=_params(2),
        name="gmlp_in",
    )(xb, w_in)


def _sgu_out_kernel(u_ref, v_ref, lng_ref, lnb_ref, ws_ref, bs_ref, wo_ref, x_ref, g_ref, b_ref,
                    xo_ref, xob_ref, p_ref, mu_ref, rs_ref, *, alpha, gd):
    grp = pl.program_id(1)
    acc_ref = xo_ref
    tm = u_ref.shape[0]

    @pl.when(grp == 0)
    def _():
        v = v_ref[...].astype(F32)
        mu = jnp.mean(v, axis=-1, keepdims=True)
        d = v - mu
        mu_ref[...] = mu
        rs_ref[...] = lax.rsqrt(jnp.mean(d * d, axis=-1, keepdims=True) + LN_EPS)
        acc_ref[...] = jnp.zeros_like(acc_ref)

    rows = lax.broadcasted_iota(jnp.int32, (GM_BLOCK, GM_BLOCK), 0) // CHUNK
    cols = lax.broadcasted_iota(jnp.int32, (GM_BLOCK, GM_BLOCK), 1) // CHUNK

    for gi in range(SGU_GROUPS_PER_STEP):
        off = pl.multiple_of((grp * SGU_GROUPS_PER_STEP + gi) * gd, V7X_LANES)
        cs = slice(gi * gd, (gi + 1) * gd)
        vg = v_ref[:, pl.ds(off, gd)].astype(F32)
        vn = ((vg - mu_ref[...]) * rs_ref[...] * lng_ref[:, cs] + lnb_ref[:, cs]).astype(BF16)
        wm = jnp.where(cols <= rows, ws_ref[gi], 0.0).astype(BF16)
        bs = bs_ref[gi]
        for r in range(tm // GM_BLOCK):
            sl = slice(r * GM_BLOCK, (r + 1) * GM_BLOCK)
            s = jnp.dot(wm, vn[sl, :], preferred_element_type=F32) + bs
            p_ref[sl, cs] = (u_ref[sl, cs].astype(F32) * s).astype(BF16)
    acc_ref[...] += jnp.dot(p_ref[...], wo_ref[...], preferred_element_type=F32)

    @pl.when(grp == pl.num_programs(1) - 1)
    def _():
        _store_residual_ln(x_ref, acc_ref[...], g_ref, b_ref, xo_ref, xob_ref, alpha)


def _sgu_out(z, ln_g, ln_b, w_s, b_s, w_out, x, g, b, slot, layer, alpha, tm):
    t, two_gh = z.shape
    gh = two_gh // 2
    gd = gh // GM_GROUPS
    d = x.shape[1]
    gps = SGU_GROUPS_PER_STEP
    kern = functools.partial(_sgu_out_kernel, alpha=alpha, gd=gd)
    return pl.pallas_call(
        kern,
        grid=(t // tm, GM_GROUPS // gps),
        in_specs=[
            pl.BlockSpec((tm, gps * gd), lambda i, k: (i, k)),
            pl.BlockSpec((tm, gh), lambda i, k: (i, 1)),
            pl.BlockSpec((None, 1, gps * gd), lambda i, k: (slot, 0, k)),
            pl.BlockSpec((None, 1, gps * gd), lambda i, k: (slot, 0, k)),
            pl.BlockSpec((None, gps, GM_BLOCK, GM_BLOCK), lambda i, k: (slot, k, 0, 0)),
            pl.BlockSpec((None, gps, GM_BLOCK, 1), lambda i, k: (slot, k, 0, 0)),
            pl.BlockSpec((None, gps * gd, d), lambda i, k: (slot, k, 0)),
            pl.BlockSpec((tm, d), lambda i, k: (i, 0)),
            pl.BlockSpec((None, 1, d), lambda i, k: (layer, 0, 0)),
            pl.BlockSpec((None, 1, d), lambda i, k: (layer, 0, 0)),
        ],
        out_specs=[
            pl.BlockSpec((tm, d), lambda i, k: (i, 0)),
            pl.BlockSpec((tm, d), lambda i, k: (i, 0)),
        ],
        out_shape=[jax.ShapeDtypeStruct((t, d), F32), jax.ShapeDtypeStruct((t, d), BF16)],
        scratch_shapes=[
            pltpu.VMEM((tm, gps * gd), BF16),
            pltpu.VMEM((tm, 1), F32),
            pltpu.VMEM((tm, 1), F32),
        ],
        compiler_params=_params(2),
        name="sgu_out",
    )(z, z, ln_g, ln_b, w_s, b_s, w_out, x, g, b)


def _ffn_kernel(xb_ref, wu_ref, cw_ref, cb_ref, wd_ref, x_ref, g_ref, b_ref,
                xo_ref, xob_ref, acc_ref, h0_ref, h1_ref, carry_ref, *, alpha, tiles_per_seq, n_ff):
    s = pl.program_id(0)

    @pl.when(s == 0)
    def _():
        h1_ref[...] = jnp.zeros_like(h1_ref)
        acc_ref[...] = jnp.zeros_like(acc_ref)
        carry_ref[...] = jnp.zeros_like(carry_ref)

    stage = functools.partial(
        _ffn_stage, xb_ref, wu_ref, cw_ref, cb_ref, wd_ref, x_ref, g_ref, b_ref, xo_ref, xob_ref,
        acc_ref, carry_ref, alpha=alpha, tiles_per_seq=tiles_per_seq, n_ff=n_ff)
    pl.when(s % 2 == 0)(functools.partial(stage, h1_ref, h0_ref))
    pl.when(s % 2 == 1)(functools.partial(stage, h0_ref, h1_ref))


def _ffn_stage(xb_ref, wu_ref, cw_ref, cb_ref, wd_ref, x_ref, g_ref, b_ref, xo_ref, xob_ref,
               acc_ref, carry_ref, h_rd_ref, h_wr_ref, *, alpha, tiles_per_seq, n_ff):
    s = pl.program_id(0)
    tm = xb_ref.shape[0]
    tn = wd_ref.shape[0]
    lag = jnp.maximum(s - 1, 0)
    ip = lag // n_ff
    jp = lag % n_ff

    h_wr_ref[...] = jnp.dot(xb_ref[...], wu_ref[...], preferred_element_type=F32)

    h = h_rd_ref[...]

    first = (ip % tiles_per_seq) == 0
    prev = jnp.where(first, 0.0, carry_ref[jp])
    carry_ref[jp] = h[tm - V7X_SUBLANES:, :]

    head_rows = lax.broadcasted_iota(jnp.int32, (V7X_SUBLANES, 2 * tn), 0)
    cw = cw_ref[...]
    out = cw[2:3, :] * h + cb_ref[...]
    for shift in (1, 2):
        hs = pltpu.roll(h, shift, 0)
        head = jnp.where(head_rows < shift, pltpu.roll(prev, shift, 0), hs[:V7X_SUBLANES, :])
        hs = jnp.concatenate([head, hs[V7X_SUBLANES:, :]], axis=0)
        out = out + cw[2 - shift:3 - shift, :] * hs

    a = out[:, :tn]
    gt = out[:, tn:]
    act = (gt * (1.0 / (1.0 + jnp.exp2(gt * (-LOG2_E)))) * a).astype(BF16)
    acc_ref[...] += jnp.dot(act, wd_ref[...], preferred_element_type=F32)

    @pl.when(s == 0)
    def _():
        acc_ref[...] = jnp.zeros_like(acc_ref)

    @pl.when((s > 0) & (jp == n_ff - 1))
    def _():
        _store_residual_ln(x_ref, acc_ref[...], g_ref, b_ref, xo_ref, xob_ref, alpha)
        acc_ref[...] = jnp.zeros_like(acc_ref)


def _conv_ffn(xb, w_up, conv_w, conv_b, w_down, x, g, b, layer, alpha, seq, tm, tn):
    t, d = x.shape
    n_ff = w_down.shape[1] // tn
    n_steps = (t // tm) * n_ff
    kern = functools.partial(_ffn_kernel, alpha=alpha, tiles_per_seq=seq // tm, n_ff=n_ff)

    def cur(s):
        c = jnp.minimum(s, n_steps - 1)
        return c // n_ff, c % n_ff

    def lag(s):
        c = jnp.maximum(s - 1, 0)
        return c // n_ff, c % n_ff

    return pl.pallas_call(
        kern,
        grid=(n_steps + 1,),
        in_specs=[
            pl.BlockSpec((tm, d), lambda s: (cur(s)[0], 0)),
            pl.BlockSpec((None, d, 2 * tn), lambda s: (layer, 0, cur(s)[1])),
            pl.BlockSpec((None, CONV_W, 2 * tn), lambda s: (layer, 0, lag(s)[1])),
            pl.BlockSpec((None, 1, 2 * tn), lambda s: (layer, 0, lag(s)[1])),
            pl.BlockSpec((None, tn, d), lambda s: (layer, lag(s)[1], 0)),
            pl.BlockSpec((tm, d), lambda s: (lag(s)[0], 0)),
            pl.BlockSpec((None, 1, d), lambda s: (layer, 0, 0)),
            pl.BlockSpec((None, 1, d), lambda s: (layer, 0, 0)),
        ],
        out_specs=[
            pl.BlockSpec((tm, d), lambda s: (lag(s)[0], 0)),
            pl.BlockSpec((tm, d), lambda s: (lag(s)[0], 0)),
        ],
        out_shape=[jax.ShapeDtypeStruct((t, d), F32), jax.ShapeDtypeStruct((t, d), BF16)],
        scratch_shapes=[
            pltpu.VMEM((tm, d), F32),
            pltpu.VMEM((tm, 2 * tn), F32),
            pltpu.VMEM((tm, 2 * tn), F32),
            pltpu.VMEM((n_ff, V7X_SUBLANES, 2 * tn), F32),
        ],
        compiler_params=_params(1),
        name="conv_ffn",
    )(xb, w_up, conv_w, conv_b, w_down, x, g, b)


def _rope_pad(x, t0, t1, t2):
    half = ROPE_DIM // 2
    return x * t0 + pltpu.roll(x, half, 1) * t1 + pltpu.roll(x, V7X_LANES - half, 1) * t2


def _mla_proj_kernel(xb_ref, win_ref, qg_ref, kvg_ref, wq_ref, wkv_ref, t0_ref, t1_ref, t2_ref,
                     q_ref, kc_ref, v_ref):
    h = jnp.dot(xb_ref[...], win_ref[...], preferred_element_type=F32)
    cq = _rms_norm(h[:, :Q_RANK], qg_ref[...]).astype(BF16)
    ckv = _rms_norm(h[:, Q_RANK:Q_RANK + KV_RANK], kvg_ref[...]).astype(BF16)
    t0, t1, t2 = t0_ref[...], t1_ref[...], t2_ref[...]
    kr = _rope_pad(h[:, Q_RANK + KV_RANK:], t0, t1, t2).astype(BF16)

    q = jnp.dot(cq, wq_ref[...], preferred_element_type=F32)
    kv = jnp.dot(ckv, wkv_ref[...], preferred_element_type=F32)
    ones_col = (lax.broadcasted_iota(jnp.int32, (xb_ref.shape[0], V7X_LANES), 1) == 0).astype(BF16)
    for hd in range(MLA_HEADS):
        qb = hd * QK_PAD
        kb = hd * KV_WIDTH
        q_ref[:, qb:qb + NOPE_DIM] = q[:, qb:qb + NOPE_DIM].astype(BF16)
        q_ref[:, qb + NOPE_DIM:qb + QK_PAD] = _rope_pad(
            q[:, qb + NOPE_DIM:qb + QK_PAD], t0, t1, t2).astype(BF16)
        kc_ref[:, qb:qb + NOPE_DIM] = kv[:, kb:kb + NOPE_DIM].astype(BF16)
        kc_ref[:, qb + NOPE_DIM:qb + QK_PAD] = kr
        v_ref[:, hd * V_PAD:hd * V_PAD + V_DIM] = kv[:, kb + NOPE_DIM:kb + KV_WIDTH].astype(BF16)
        v_ref[:, hd * V_PAD + V_DIM:(hd + 1) * V_PAD] = ones_col


def _mla_proj(xb, w_in, q_g, kv_g, w_q, w_kv, t0, t1, t2, slot, seq, tm):
    t, d = xb.shape
    n_in = w_in.shape[1]
    tiles_per_seq = seq // tm
    hq = MLA_HEADS * QK_PAD
    rope_spec = pl.BlockSpec((tm, V7X_LANES), lambda i: (i % tiles_per_seq, 0))
    return pl.pallas_call(
        _mla_proj_kernel,
        grid=(t // tm,),
        in_specs=[
            pl.BlockSpec((tm, d), lambda i: (i, 0)),
            pl.BlockSpec((d, n_in), lambda i: (0, 0)),
            pl.BlockSpec((None, 1, Q_RANK), lambda i: (slot, 0, 0)),
            pl.BlockSpec((None, 1, KV_RANK), lambda i: (slot, 0, 0)),
            pl.BlockSpec((Q_RANK, hq), lambda i: (0, 0)),
            pl.BlockSpec((None, KV_RANK, MLA_HEADS * KV_WIDTH), lambda i: (slot, 0, 0)),
            rope_spec, rope_spec, rope_spec,
        ],
        out_specs=[
            pl.BlockSpec((tm, hq), lambda i: (i, 0)),
            pl.BlockSpec((tm, hq), lambda i: (i, 0)),
            pl.BlockSpec((tm, MLA_HEADS * V_PAD), lambda i: (i, 0)),
        ],
        out_shape=[
            jax.ShapeDtypeStruct((t, hq), BF16),
            jax.ShapeDtypeStruct((t, hq), BF16),
            jax.ShapeDtypeStruct((t, MLA_HEADS * V_PAD), BF16),
        ],
        compiler_params=_params(1),
        name="mla_proj",
    )(xb, w_in, q_g, kv_g, w_q, w_kv, t0, t1, t2)


def _attn_kernel(q_ref, k_ref, v_ref, o_ref, *, tk):
    qi = pl.program_id(2)
    tq = q_ref.shape[0]
    c = SM_SCALE * LOG2_E
    qs = [q_ref[:, hd * QK_PAD:(hd + 1) * QK_PAD] for hd in range(ATTN_HEADS_PER_STEP)]

    def step(kb, carry, diagonal):
        off = pl.multiple_of(kb * tk, tk)
        new = []
        for hd in range(ATTN_HEADS_PER_STEP):
            m, acc = carry[hd]
            k = k_ref[pl.ds(off, tk), hd * QK_PAD:(hd + 1) * QK_PAD]
            v = v_ref[pl.ds(off, tk), hd * V_PAD:(hd + 1) * V_PAD]
            s = lax.dot_general(qs[hd], k, (((1,), (1,)), ((), ())), preferred_element_type=F32)
            if diagonal:
                rows = lax.broadcasted_iota(jnp.int32, (tq, tk), 0) // CHUNK
                cols = lax.broadcasted_iota(jnp.int32, (tq, tk), 1) // CHUNK
                s = jnp.where(cols <= rows, s, -jnp.inf)
            m_new = jnp.maximum(m, jnp.max(s, axis=-1, keepdims=True))
            a = jnp.exp2((m - m_new) * c)
            p = jnp.exp2((s - m_new) * c)
            acc = a * acc + jnp.dot(p.astype(BF16), v, preferred_element_type=F32)
            new.append((m_new, acc))
        return tuple(new)

    init = tuple(
        (jnp.full((tq, 1), -jnp.inf, F32), jnp.zeros((tq, V_PAD), F32))
        for _ in range(ATTN_HEADS_PER_STEP))
    carry = lax.fori_loop(0, qi, lambda kb, cr: step(kb, cr, False), init)
    carry = step(qi, carry, True)
    for hd in range(ATTN_HEADS_PER_STEP):
        _, acc = carry[hd]
        o_ref[:, hd * V_DIM:(hd + 1) * V_DIM] = (
            acc[:, :V_DIM] / acc[:, V_DIM:V_DIM + 1]).astype(o_ref.dtype)


def _attention(q, kc, v, batch, seq, tq):
    t = q.shape[0]
    nq = seq // tq
    hps = ATTN_HEADS_PER_STEP
    kern = functools.partial(_attn_kernel, tk=tq)
    return pl.pallas_call(
        kern,
        grid=(batch, MLA_HEADS // hps, nq),
        in_specs=[
            pl.BlockSpec((tq, hps * QK_PAD), lambda b, h, i: (b * nq + i, h)),
            pl.BlockSpec((seq, hps * QK_PAD), lambda b, h, i: (b, h)),
            pl.BlockSpec((seq, hps * V_PAD), lambda b, h, i: (b, h)),
        ],
        out_specs=pl.BlockSpec((tq, hps * V_DIM), lambda b, h, i: (b * nq + i, h)),
        out_shape=jax.ShapeDtypeStruct((t, MLA_HEADS * V_DIM), BF16),
        compiler_params=_params(3),
        name="attention",
    )(q, kc, v)


def _attn_out_kernel(o_ref, w_ref, x_ref, g_ref, b_ref, xo_ref, xob_ref, *, alpha):
    m = jnp.dot(o_ref[...], w_ref[...], preferred_element_type=F32)
    _store_residual_ln(x_ref, m, g_ref, b_ref, xo_ref, xob_ref, alpha)


def _attn_out(o, w_out, x, g, b, slot, layer, alpha, tm):
    t, d = x.shape
    k = o.shape[1]
    kern = functools.partial(_attn_out_kernel, alpha=alpha)
    return pl.pallas_call(
        kern,
        grid=(t // tm,),
        in_specs=[
            pl.BlockSpec((tm, k), lambda i: (i, 0)),
            pl.BlockSpec((None, k, d), lambda i: (slot, 0, 0)),
            pl.BlockSpec((tm, d), lambda i: (i, 0)),
            pl.BlockSpec((None, 1, d), lambda i: (layer, 0, 0)),
            pl.BlockSpec((None, 1, d), lambda i: (layer, 0, 0)),
        ],
        out_specs=[
            pl.BlockSpec((tm, d), lambda i: (i, 0)),
            pl.BlockSpec((tm, d), lambda i: (i, 0)),
        ],
        out_shape=[jax.ShapeDtypeStruct((t, d), F32), jax.ShapeDtypeStruct((t, d), BF16)],
        compiler_params=_params(1),
        name="attn_out",
    )(o, w_out, x, g, b)


def _round_up(n, m):
    return (n + m - 1) // m * m


def _ffn_segments(d_ff, tn):
    segs = []
    for j in range(_round_up(d_ff, tn) // tn):
        width = min(tn, d_ff - j * tn)
        for half in range(2):
            dst = (2 * j + half) * tn
            segs.append((dst, half * d_ff + j * tn, width))
            if width < tn:
                segs.append((dst + width, None, tn - width))
    return segs


def _interleave_halves(w, d_ff, tn):
    parts = []
    for dst, src, width in _ffn_segments(d_ff, tn):
        parts.append(jnp.zeros(w.shape[:-1] + (width,), w.dtype) if src is None
                     else w[..., src:src + width])
    return jnp.concatenate(parts, axis=-1)


def _rope_tables(seq):
    half = ROPE_DIM // 2
    inv_freq = ROPE_THETA ** (-jnp.arange(half, dtype=F32) / half)
    ang = jnp.arange(seq, dtype=F32)[:, None] * inv_freq[None, :]
    cos, sin = jnp.cos(ang), jnp.sin(ang)
    z = jnp.zeros_like(cos)
    zz = jnp.zeros((seq, V7X_LANES - ROPE_DIM), F32)
    t0 = jnp.concatenate([cos, cos, zz], axis=1)
    t1 = jnp.concatenate([z, sin, zz], axis=1)
    t2 = jnp.concatenate([-sin, z, zz], axis=1)
    return t0, t1, t2


def _pad_heads(w, width):
    k = w.shape[0]
    w = w.reshape(k, MLA_HEADS, width)
    w = jnp.pad(w, ((0, 0), (0, 0), (0, QK_PAD - width)))
    return w.reshape(k, MLA_HEADS * QK_PAD)


def kernel(x, gm_w_in, gm_ln_g, gm_ln_b, gm_w_s, gm_b_s, gm_w_out, mla_w_in, mla_q_norm_g, mla_kv_norm_g, mla_w_q_b, mla_w_kv_b, mla_w_out, ffn_w_up, ffn_conv_w, ffn_conv_b, ffn_w_down, ln_mix_g, ln_mix_b, ln_ffn_g, ln_ffn_b):
    batch, seq, d = x.shape
    t = batch * seq
    depth = ffn_w_up.shape[0]
    d_ff = ffn_w_down.shape[1]
    alpha = float((2 * depth) ** 0.25)

    tm = min(512, seq)
    tm_in = min(1024, seq)
    tn_in = 2048
    tn_ff = 512
    tq = min(512, seq)
    assert seq % tm == 0 and seq % tm_in == 0 and seq % tq == 0 and tm % GM_BLOCK == 0
    assert (2 * gm_w_out.shape[1]) % tn_in == 0 and tq % CHUNK == 0
    ffp = _round_up(d_ff, tn_ff)

    gm_w_in_b = _cast_weights(gm_w_in, tr=256)
    gm_w_out_b = _cast_weights(gm_w_out, tr=1024)
    mla_w_kv_b = _cast_weights(mla_w_kv_b, tr=KV_RANK)
    mla_w_out_b = _cast_weights(mla_w_out, tr=1024)
    ffn_w_up_b = _cast_weights(ffn_w_up, tr=256, segments=_ffn_segments(d_ff, tn_ff), n_out=2 * ffp)
    ffn_w_down_b = _cast_weights(ffn_w_down, tr=V7X_LANES, k_out=ffp)
    conv_w = _interleave_halves(ffn_conv_w, d_ff, tn_ff)
    conv_b = _interleave_halves(ffn_conv_b[:, None, :], d_ff, tn_ff)

    gm_ln_g3, gm_ln_b3 = gm_ln_g[:, None, :], gm_ln_b[:, None, :]
    gm_b_s4 = gm_b_s[:, :, :, None]
    q_g3, kv_g3 = mla_q_norm_g[:, None, :], mla_kv_norm_g[:, None, :]
    mix_g3, mix_b3 = ln_mix_g[:, None, :], ln_mix_b[:, None, :]
    ffn_g3, ffn_b3 = ln_ffn_g[:, None, :], ln_ffn_b[:, None, :]

    xf = x.reshape(t, d)
    xb = xf
    t0, t1, t2 = _rope_tables(seq)

    for i in range(depth):
        slot = i // 2
        if i % 2 == 0:
            z = _gmlp_in(xb, gm_w_in_b, slot, tm_in, tn_in)
            xf, xb = _sgu_out(z, gm_ln_g3, gm_ln_b3, gm_w_s, gm_b_s4, gm_w_out_b, xf,
                              mix_g3, mix_b3, slot, i, alpha, tm)
        else:
            w_in = jnp.pad(mla_w_in[slot], ((0, 0), (0, V7X_LANES - ROPE_DIM))).astype(BF16)
            w_q = _pad_heads(mla_w_q_b[slot], NOPE_DIM + ROPE_DIM).astype(BF16)
            q, kc, v = _mla_proj(xb, w_in, q_g3, kv_g3, w_q, mla_w_kv_b, t0, t1, t2, slot, seq, tm)
            o = _attention(q, kc, v, batch, seq, tq)
            xf, xb = _attn_out(o, mla_w_out_b, xf, mix_g3, mix_b3, slot, i, alpha, tm)

        xf, xb = _conv_ffn(xb, ffn_w_up_b, conv_w, conv_b, ffn_w_down_b, xf, ffn_g3, ffn_b3,
                           i, alpha, seq, tm, tn_ff)

    return xf.reshape(batch, seq, d)
```

```python
import functools

import jax
import jax.numpy as jnp
from jax import lax
from jax.experimental import pallas as pl
from jax.experimental.pallas import tpu as pltpu

CHUNK = 64
GM_BLOCK = 128
GM_GROUPS = 8
MLA_HEADS = 16
Q_RANK = 512
KV_RANK = 512
NOPE_DIM = 128
ROPE_DIM = 64
V_DIM = 128
ROPE_THETA = 10000.0
SM_SCALE = (NOPE_DIM + ROPE_DIM) ** -0.5
LOG2_E = 1.4426950408889634
CONV_W = 3
LN_EPS = 1e-5
RMS_EPS = 1e-6

V7X_LANES = 128
V7X_SUBLANES = 8
V7X_VMEM_BYTES = 64 * 1024 * 1024
VMEM_LIMIT_BYTES = V7X_VMEM_BYTES - 6 * 1024 * 1024

QK_PAD = 2 * V7X_LANES
KV_WIDTH = NOPE_DIM + V_DIM
V_PAD = V_DIM + V7X_LANES
ATTN_HEADS_PER_STEP = 4
SGU_GROUPS_PER_STEP = 2

F32 = jnp.float32
BF16 = jnp.bfloat16


def _params(n_axes):
    return pltpu.CompilerParams(
        dimension_semantics=("arbitrary",) * n_axes,
        vmem_limit_bytes=VMEM_LIMIT_BYTES,
    )


def _gelu_tanh(x):
    c = (2.0 / jnp.pi) ** 0.5
    half_x = 0.5 * x
    return half_x + half_x * jnp.tanh(x * (c + (c * 0.044715) * (x * x)))


def _layer_norm(y, g, b):
    mu = jnp.mean(y, axis=-1, keepdims=True)
    d = y - mu
    var = jnp.mean(d * d, axis=-1, keepdims=True)
    return d * lax.rsqrt(var + LN_EPS) * g + b


def _rms_norm(y, g):
    return y * lax.rsqrt(jnp.mean(y * y, axis=-1, keepdims=True) + RMS_EPS) * g


def _store_residual_ln(x_ref, m, g_ref, b_ref, xo_ref, xob_ref, alpha):
    y = _layer_norm(alpha * x_ref[...] + m, g_ref[...], b_ref[...])
    xo_ref[...] = y
    xob_ref[...] = y.astype(BF16)


def _cast_kernel(w_ref, o_ref, *, segments, n_in_blocks, n_out_blocks):
    def copy():
        for dst, src, width in segments:
            if src is None:
                o_ref[:, dst:dst + width] = jnp.zeros((o_ref.shape[0], width), o_ref.dtype)
            else:
                o_ref[:, dst:dst + width] = w_ref[:, src:src + width].astype(o_ref.dtype)

    if n_in_blocks == n_out_blocks:
        copy()
    else:
        r = pl.program_id(1)
        pl.when(r < n_in_blocks)(copy)

        @pl.when(r >= n_in_blocks)
        def _():
            o_ref[...] = jnp.zeros_like(o_ref)


def _cast_weights(w, tr, segments=None, n_out=None, k_out=None):
    n_layers, k, n = w.shape
    n_out = n if n_out is None else n_out
    k_out = k if k_out is None else k_out
    segments = ((0, 0, n),) if segments is None else tuple(segments)
    assert k % tr == 0 and k_out % tr == 0
    n_in_blocks, n_out_blocks = k // tr, k_out // tr
    kern = functools.partial(_cast_kernel, segments=segments, n_in_blocks=n_in_blocks,
                             n_out_blocks=n_out_blocks)
    return pl.pallas_call(
        kern,
        grid=(n_layers, n_out_blocks),
        in_specs=[pl.BlockSpec((None, tr, n), lambda l, r: (l, jnp.minimum(r, n_in_blocks - 1), 0))],
        out_specs=pl.BlockSpec((None, tr, n_out), lambda l, r: (l, r, 0)),
        out_shape=jax.ShapeDtypeStruct((n_layers, k_out, n_out), BF16),
        compiler_params=_params(2),
        name="cast_weights",
    )(w)


def _gmlp_in_kernel(x_ref, w_ref, z_ref):
    h = jnp.dot(x_ref[...].astype(BF16), w_ref[...], preferred_element_type=F32)
    z_ref[...] = _gelu_tanh(h).astype(z_ref.dtype)


def _gmlp_in(xb, w_in, slot, tm, tn):
    t, d = xb.shape
    n = w_in.shape[2]
    return pl.pallas_call(
        _gmlp_in_kernel,
        grid=(t // tm, n // tn),
        in_specs=[
            pl.BlockSpec((tm, d), lambda i, j: (i, 0)),
            pl.BlockSpec((None, d, tn), lambda i, j: (slot, 0, j)),
        ],
        out_specs=pl.BlockSpec((tm, tn), lambda i, j: (i, j)),
        out_shape=jax.ShapeDtypeStruct((t, n), BF16),
        compiler_params=_params(2),
        name="gmlp_in",
    )(xb, w_in)


def _sgu_out_kernel(u_ref, v_ref, lng_ref, lnb_ref, ws_ref, bs_ref, wo_ref, x_ref, g_ref, b_ref,
                    xo_ref, xob_ref, p_ref, mu_ref, rs_ref, *, alpha, gd):
    grp = pl.program_id(1)
    acc_ref = xo_ref
    tm = u_ref.shape[0]

    @pl.when(grp == 0)
    def _():
        v = v_ref[...].astype(F32)
        mu = jnp.mean(v, axis=-1, keepdims=True)
        d = v - mu
        mu_ref[...] = mu
        rs_ref[...] = lax.rsqrt(jnp.mean(d * d, axis=-1, keepdims=True) + LN_EPS)
        acc_ref[...] = jnp.zeros_like(acc_ref)

    rows = lax.broadcasted_iota(jnp.int32, (GM_BLOCK, GM_BLOCK), 0) // CHUNK
    cols = lax.broadcasted_iota(jnp.int32, (GM_BLOCK, GM_BLOCK), 1) // CHUNK

    for gi in range(SGU_GROUPS_PER_STEP):
        off = pl.multiple_of((grp * SGU_GROUPS_PER_STEP + gi) * gd, V7X_LANES)
        cs = slice(gi * gd, (gi + 1) * gd)
        vg = v_ref[:, pl.ds(off, gd)].astype(F32)
        vn = ((vg - mu_ref[...]) * rs_ref[...] * lng_ref[:, cs] + lnb_ref[:, cs]).astype(BF16)
        wm = jnp.where(cols <= rows, ws_ref[gi], 0.0).astype(BF16)
        bs = bs_ref[gi]
        for r in range(tm // GM_BLOCK):
            sl = slice(r * GM_BLOCK, (r + 1) * GM_BLOCK)
            s = jnp.dot(wm, vn[sl, :], preferred_element_type=F32) + bs
            p_ref[sl, cs] = (u_ref[sl, cs].astype(F32) * s).astype(BF16)
    acc_ref[...] += jnp.dot(p_ref[...], wo_ref[...], preferred_element_type=F32)

    @pl.when(grp == pl.num_programs(1) - 1)
    def _():
        _store_residual_ln(x_ref, acc_ref[...], g_ref, b_ref, xo_ref, xob_ref, alpha)


def _sgu_out(z, ln_g, ln_b, w_s, b_s, w_out, x, g, b, slot, layer, alpha, tm):
    t, two_gh = z.shape
    gh = two_gh // 2
    gd = gh // GM_GROUPS
    d = x.shape[1]
    gps = SGU_GROUPS_PER_STEP
    kern = functools.partial(_sgu_out_kernel, alpha=alpha, gd=gd)
    return pl.pallas_call(
        kern,
        grid=(t // tm, GM_GROUPS // gps),
        in_specs=[
            pl.BlockSpec((tm, gps * gd), lambda i, k: (i, k)),
            pl.BlockSpec((tm, gh), lambda i, k: (i, 1)),
            pl.BlockSpec((None, 1, gps * gd), lambda i, k: (slot, 0, k)),
            pl.BlockSpec((None, 1, gps * gd), lambda i, k: (slot, 0, k)),
            pl.BlockSpec((None, gps, GM_BLOCK, GM_BLOCK), lambda i, k: (slot, k, 0, 0)),
            pl.BlockSpec((None, gps, GM_BLOCK, 1), lambda i, k: (slot, k, 0, 0)),
            pl.BlockSpec((None, gps * gd, d), lambda i, k: (slot, k, 0)),
            pl.BlockSpec((tm, d), lambda i, k: (i, 0)),
            pl.BlockSpec((None, 1, d), lambda i, k: (layer, 0, 0)),
            pl.BlockSpec((None, 1, d), lambda i, k: (layer, 0, 0)),
        ],
        out_specs=[
            pl.BlockSpec((tm, d), lambda i, k: (i, 0)),
            pl.BlockSpec((tm, d), lambda i, k: (i, 0)),
        ],
        out_shape=[jax.ShapeDtypeStruct((t, d), F32), jax.ShapeDtypeStruct((t, d), BF16)],
        scratch_shapes=[
            pltpu.VMEM((tm, gps * gd), BF16),
            pltpu.VMEM((tm, 1), F32),
            pltpu.VMEM((tm, 1), F32),
        ],
        compiler_params=_params(2),
        name="sgu_out",
    )(z, z, ln_g, ln_b, w_s, b_s, w_out, x, g, b)


def _ffn_kernel(xb_ref, wu_ref, cw_ref, cb_ref, wd_ref, x_ref, g_ref, b_ref,
                xo_ref, xob_ref, acc_ref, h0_ref, h1_ref, carry_ref, *, alpha, tiles_per_seq, n_ff):
    s = pl.program_id(0)

    @pl.when(s == 0)
    def _():
        h1_ref[...] = jnp.zeros_like(h1_ref)
        acc_ref[...] = jnp.zeros_like(acc_ref)
        carry_ref[...] = jnp.zeros_like(carry_ref)

    stage = functools.partial(
        _ffn_stage, xb_ref, wu_ref, cw_ref, cb_ref, wd_ref, x_ref, g_ref, b_ref, xo_ref, xob_ref,
        acc_ref, carry_ref, alpha=alpha, tiles_per_seq=tiles_per_seq, n_ff=n_ff)
    pl.when(s % 2 == 0)(functools.partial(stage, h1_ref, h0_ref))
    pl.when(s % 2 == 1)(functools.partial(stage, h0_ref, h1_ref))


def _ffn_stage(xb_ref, wu_ref, cw_ref, cb_ref, wd_ref, x_ref, g_ref, b_ref, xo_ref, xob_ref,
               acc_ref, carry_ref, h_rd_ref, h_wr_ref, *, alpha, tiles_per_seq, n_ff):
    s = pl.program_id(0)
    tm = xb_ref.shape[0]
    tn = wd_ref.shape[0]
    lag = jnp.maximum(s - 1, 0)
    ip = lag // n_ff
    jp = lag % n_ff

    h_wr_ref[...] = jnp.dot(xb_ref[...], wu_ref[...], preferred_element_type=F32)

    h = h_rd_ref[...]

    first = (ip % tiles_per_seq) == 0
    prev = jnp.where(first, 0.0, carry_ref[jp])
    carry_ref[jp] = h[tm - V7X_SUBLANES:, :]

    head_rows = lax.broadcasted_iota(jnp.int32, (V7X_SUBLANES, 2 * tn), 0)
    cw = cw_ref[...]
    out = cw[2:3, :] * h + cb_ref[...]
    for shift in (1, 2):
        hs = pltpu.roll(h, shift, 0)
        head = jnp.where(head_rows < shift, pltpu.roll(prev, shift, 0), hs[:V7X_SUBLANES, :])
        hs = jnp.concatenate([head, hs[V7X_SUBLANES:, :]], axis=0)
        out = out + cw[2 - shift:3 - shift, :] * hs

    a = out[:, :tn]
    gt = out[:, tn:]
    act = (gt * (1.0 / (1.0 + jnp.exp2(gt * (-LOG2_E)))) * a).astype(BF16)
    acc_ref[...] += jnp.dot(act, wd_ref[...], preferred_element_type=F32)

    @pl.when(s == 0)
    def _():
        acc_ref[...] = jnp.zeros_like(acc_ref)

    @pl.when((s > 0) & (jp == n_ff - 1))
    def _():
        _store_residual_ln(x_ref, acc_ref[...], g_ref, b_ref, xo_ref, xob_ref, alpha)
        acc_ref[...] = jnp.zeros_like(acc_ref)


def _conv_ffn(xb, w_up, conv_w, conv_b, w_down, x, g, b, layer, alpha, seq, tm, tn):
    t, d = x.shape
    n_ff = w_down.shape[1] // tn
    n_steps = (t // tm) * n_ff
    kern = functools.partial(_ffn_kernel, alpha=alpha, tiles_per_seq=seq // tm, n_ff=n_ff)

    def cur(s):
        c = jnp.minimum(s, n_steps - 1)
        return c // n_ff, c % n_ff

    def lag(s):
        c = jnp.maximum(s - 1, 0)
        return c // n_ff, c % n_ff

    return pl.pallas_call(
        kern,
        grid=(n_steps + 1,),
        in_specs=[
            pl.BlockSpec((tm, d), lambda s: (cur(s)[0], 0)),
            pl.BlockSpec((None, d, 2 * tn), lambda s: (layer, 0, cur(s)[1])),
            pl.BlockSpec((None, CONV_W, 2 * tn), lambda s: (layer, 0, lag(s)[1])),
            pl.BlockSpec((None, 1, 2 * tn), lambda s: (layer, 0, lag(s)[1])),
            pl.BlockSpec((None, tn, d), lambda s: (layer, lag(s)[1], 0)),
            pl.BlockSpec((tm, d), lambda s: (lag(s)[0], 0)),
            pl.BlockSpec((None, 1, d), lambda s: (layer, 0, 0)),
            pl.BlockSpec((None, 1, d), lambda s: (layer, 0, 0)),
        ],
        out_specs=[
            pl.BlockSpec((tm, d), lambda s: (lag(s)[0], 0)),
            pl.BlockSpec((tm, d), lambda s: (lag(s)[0], 0)),
        ],
        out_shape=[jax.ShapeDtypeStruct((t, d), F32), jax.ShapeDtypeStruct((t, d), BF16)],
        scratch_shapes=[
            pltpu.VMEM((tm, d), F32),
            pltpu.VMEM((tm, 2 * tn), F32),
            pltpu.VMEM((tm, 2 * tn), F32),
            pltpu.VMEM((n_ff, V7X_SUBLANES, 2 * tn), F32),
        ],
        compiler_params=_params(1),
        name="conv_ffn",
    )(xb, w_up, conv_w, conv_b, w_down, x, g, b)


def _rope_pad(x, t0, t1, t2):
    half = ROPE_DIM // 2
    return x * t0 + pltpu.roll(x, half, 1) * t1 + pltpu.roll(x, V7X_LANES - half, 1) * t2


def _mla_proj_kernel(xb_ref, win_ref, qg_ref, kvg_ref, wq_ref, wkv_ref, t0_ref, t1_ref, t2_ref,
                     q_ref, kc_ref, v_ref):
    h = jnp.dot(xb_ref[...], win_ref[...], preferred_element_type=F32)
    cq = _rms_norm(h[:, :Q_RANK], qg_ref[...]).astype(BF16)
    ckv = _rms_norm(h[:, Q_RANK:Q_RANK + KV_RANK], kvg_ref[...]).astype(BF16)
    t0, t1, t2 = t0_ref[...], t1_ref[...], t2_ref[...]
    kr = _rope_pad(h[:, Q_RANK + KV_RANK:], t0, t1, t2).astype(BF16)

    q = jnp.dot(cq, wq_ref[...], preferred_element_type=F32)
    kv = jnp.dot(ckv, wkv_ref[...], preferred_element_type=F32)
    ones_col = (lax.broadcasted_iota(jnp.int32, (xb_ref.shape[0], V7X_LANES), 1) == 0).astype(BF16)
    for hd in range(MLA_HEADS):
        qb = hd * QK_PAD
        kb = hd * KV_WIDTH
        q_ref[:, qb:qb + NOPE_DIM] = q[:, qb:qb + NOPE_DIM].astype(BF16)
        q_ref[:, qb + NOPE_DIM:qb + QK_PAD] = _rope_pad(
            q[:, qb + NOPE_DIM:qb + QK_PAD], t0, t1, t2).astype(BF16)
        kc_ref[:, qb:qb + NOPE_DIM] = kv[:, kb:kb + NOPE_DIM].astype(BF16)
        kc_ref[:, qb + NOPE_DIM:qb + QK_PAD] = kr
        v_ref[:, hd * V_PAD:hd * V_PAD + V_DIM] = kv[:, kb + NOPE_DIM:kb + KV_WIDTH].astype(BF16)
        v_ref[:, hd * V_PAD + V_DIM:(hd + 1) * V_PAD] = ones_col


def _mla_proj(xb, w_in, q_g, kv_g, w_q, w_kv, t0, t1, t2, slot, seq, tm):
    t, d = xb.shape
    n_in = w_in.shape[1]
    tiles_per_seq = seq // tm
    hq = MLA_HEADS * QK_PAD
    rope_spec = pl.BlockSpec((tm, V7X_LANES), lambda i: (i % tiles_per_seq, 0))
    return pl.pallas_call(
        _mla_proj_kernel,
        grid=(t // tm,),
        in_specs=[
            pl.BlockSpec((tm, d), lambda i: (i, 0)),
            pl.BlockSpec((d, n_in), lambda i: (0, 0)),
            pl.BlockSpec((None, 1, Q_RANK), lambda i: (slot, 0, 0)),
            pl.BlockSpec((None, 1, KV_RANK), lambda i: (slot, 0, 0)),
            pl.BlockSpec((Q_RANK, hq), lambda i: (0, 0)),
            pl.BlockSpec((None, KV_RANK, MLA_HEADS * KV_WIDTH), lambda i: (slot, 0, 0)),
            rope_spec, rope_spec, rope_spec,
        ],
        out_specs=[
            pl.BlockSpec((tm, hq), lambda i: (i, 0)),
            pl.BlockSpec((tm, hq), lambda i: (i, 0)),
            pl.BlockSpec((tm, MLA_HEADS * V_PAD), lambda i: (i, 0)),
        ],
        out_shape=[
            jax.ShapeDtypeStruct((t, hq), BF16),
            jax.ShapeDtypeStruct((t, hq), BF16),
            jax.ShapeDtypeStruct((t, MLA_HEADS * V_PAD), BF16),
        ],
        compiler_params=_params(1),
        name="mla_proj",
    )(xb, w_in, q_g, kv_g, w_q, w_kv, t0, t1, t2)


def _attn_kernel(q_ref, k_ref, v_ref, o_ref):
    qi = pl.program_id(2)
    tq = q_ref.shape[0]
    c = SM_SCALE * LOG2_E
    qs = [q_ref[:, hd * QK_PAD:(hd + 1) * QK_PAD] for hd in range(ATTN_HEADS_PER_STEP)]

    def step(off, tk, carry, diagonal):
        new = []
        for hd in range(ATTN_HEADS_PER_STEP):
            m, acc = carry[hd]
            k = k_ref[pl.ds(off, tk), hd * QK_PAD:(hd + 1) * QK_PAD]
            v = v_ref[pl.ds(off, tk), hd * V_PAD:(hd + 1) * V_PAD]
            s = lax.dot_general(qs[hd], k, (((1,), (1,)), ((), ())), preferred_element_type=F32)
            if diagonal:
                rows = lax.broadcasted_iota(jnp.int32, (tq, tk), 0) // CHUNK
                cols = lax.broadcasted_iota(jnp.int32, (tq, tk), 1) // CHUNK
                s = jnp.where(cols <= rows, s, -jnp.inf)
            m_new = jnp.maximum(m, jnp.max(s, axis=-1, keepdims=True))
            a = jnp.exp2((m - m_new) * c)
            p = jnp.exp2((s - m_new) * c)
            acc = a * acc + jnp.dot(p.astype(BF16), v, preferred_element_type=F32)
            new.append((m_new, acc))
        return tuple(new)

    init = tuple(
        (jnp.full((tq, 1), -jnp.inf, F32), jnp.zeros((tq, V_PAD), F32))
        for _ in range(ATTN_HEADS_PER_STEP))
    carry = lax.fori_loop(
        0, qi // 2, lambda t, cr: step(pl.multiple_of(t * (2 * tq), 2 * tq), 2 * tq, cr, False), init)
    carry = lax.fori_loop(
        0, qi % 2, lambda t, cr: step(pl.multiple_of((qi - 1) * tq, tq), tq, cr, False), carry)
    carry = step(pl.multiple_of(qi * tq, tq), tq, carry, True)
    for hd in range(ATTN_HEADS_PER_STEP):
        _, acc = carry[hd]
        o_ref[:, hd * V_DIM:(hd + 1) * V_DIM] = (
            acc[:, :V_DIM] / acc[:, V_DIM:V_DIM + 1]).astype(o_ref.dtype)


def _attention(q, kc, v, batch, seq, tq):
    t = q.shape[0]
    nq = seq // tq
    hps = ATTN_HEADS_PER_STEP
    return pl.pallas_call(
        _attn_kernel,
        grid=(batch, MLA_HEADS // hps, nq),
        in_specs=[
            pl.BlockSpec((tq, hps * QK_PAD), lambda b, h, i: (b * nq + i, h)),
            pl.BlockSpec((seq, hps * QK_PAD), lambda b, h, i: (b, h)),
            pl.BlockSpec((seq, hps * V_PAD), lambda b, h, i: (b, h)),
        ],
        out_specs=pl.BlockSpec((tq, hps * V_DIM), lambda b, h, i: (b * nq + i, h)),
        out_shape=jax.ShapeDtypeStruct((t, MLA_HEADS * V_DIM), BF16),
        compiler_params=_params(3),
        name="attention",
    )(q, kc, v)


def _attn_out_kernel(o_ref, w_ref, x_ref, g_ref, b_ref, xo_ref, xob_ref, *, alpha):
    m = jnp.dot(o_ref[...], w_ref[...], preferred_element_type=F32)
    _store_residual_ln(x_ref, m, g_ref, b_ref, xo_ref, xob_ref, alpha)


def _attn_out(o, w_out, x, g, b, slot, layer, alpha, tm):
    t, d = x.shape
    k = o.shape[1]
    kern = functools.partial(_attn_out_kernel, alpha=alpha)
    return pl.pallas_call(
        kern,
        grid=(t // tm,),
        in_specs=[
            pl.BlockSpec((tm, k), lambda i: (i, 0)),
            pl.BlockSpec((None, k, d), lambda i: (slot, 0, 0)),
            pl.BlockSpec((tm, d), lambda i: (i, 0)),
            pl.BlockSpec((None, 1, d), lambda i: (layer, 0, 0)),
            pl.BlockSpec((None, 1, d), lambda i: (layer, 0, 0)),
        ],
        out_specs=[
            pl.BlockSpec((tm, d), lambda i: (i, 0)),
            pl.BlockSpec((tm, d), lambda i: (i, 0)),
        ],
        out_shape=[jax.ShapeDtypeStruct((t, d), F32), jax.ShapeDtypeStruct((t, d), BF16)],
        compiler_params=_params(1),
        name="attn_out",
    )(o, w_out, x, g, b)


def _round_up(n, m):
    return (n + m - 1) // m * m


def _ffn_segments(d_ff, tn):
    segs = []
    for j in range(_round_up(d_ff, tn) // tn):
        width = min(tn, d_ff - j * tn)
        for half in range(2):
            dst = (2 * j + half) * tn
            segs.append((dst, half * d_ff + j * tn, width))
            if width < tn:
                segs.append((dst + width, None, tn - width))
    return segs


def _interleave_halves(w, d_ff, tn):
    parts = []
    for dst, src, width in _ffn_segments(d_ff, tn):
        parts.append(jnp.zeros(w.shape[:-1] + (width,), w.dtype) if src is None
                     else w[..., src:src + width])
    return jnp.concatenate(parts, axis=-1)


def _rope_tables(seq):
    half = ROPE_DIM // 2
    inv_freq = ROPE_THETA ** (-jnp.arange(half, dtype=F32) / half)
    ang = jnp.arange(seq, dtype=F32)[:, None] * inv_freq[None, :]
    cos, sin = jnp.cos(ang), jnp.sin(ang)
    z = jnp.zeros_like(cos)
    zz = jnp.zeros((seq, V7X_LANES - ROPE_DIM), F32)
    t0 = jnp.concatenate([cos, cos, zz], axis=1)
    t1 = jnp.concatenate([z, sin, zz], axis=1)
    t2 = jnp.concatenate([-sin, z, zz], axis=1)
    return t0, t1, t2


def _pad_heads(w, width):
    k = w.shape[0]
    w = w.reshape(k, MLA_HEADS, width)
    w = jnp.pad(w, ((0, 0), (0, 0), (0, QK_PAD - width)))
    return w.reshape(k, MLA_HEADS * QK_PAD)


def kernel(x, gm_w_in, gm_ln_g, gm_ln_b, gm_w_s, gm_b_s, gm_w_out, mla_w_in, mla_q_norm_g, mla_kv_norm_g, mla_w_q_b, mla_w_kv_b, mla_w_out, ffn_w_up, ffn_conv_w, ffn_conv_b, ffn_w_down, ln_mix_g, ln_mix_b, ln_ffn_g, ln_ffn_b):
    batch, seq, d = x.shape
    t = batch * seq
    depth = ffn_w_up.shape[0]
    d_ff = ffn_w_down.shape[1]
    alpha = float((2 * depth) ** 0.25)

    tm = min(512, seq)
    tm_in = min(1024, seq)
    tn_in = 2048
    tn_ff = 512
    tq = min(512, seq)
    assert seq % tm == 0 and seq % tm_in == 0 and seq % tq == 0 and tm % GM_BLOCK == 0
    assert (2 * gm_w_out.shape[1]) % tn_in == 0 and tq % CHUNK == 0
    ffp = _round_up(d_ff, tn_ff)

    gm_w_in_b = _cast_weights(gm_w_in, tr=256)
    gm_w_out_b = _cast_weights(gm_w_out, tr=1024)
    mla_w_kv_b = _cast_weights(mla_w_kv_b, tr=KV_RANK)
    mla_w_out_b = _cast_weights(mla_w_out, tr=1024)
    ffn_w_up_b = _cast_weights(ffn_w_up, tr=256, segments=_ffn_segments(d_ff, tn_ff), n_out=2 * ffp)
    ffn_w_down_b = _cast_weights(ffn_w_down, tr=V7X_LANES, k_out=ffp)
    conv_w = _interleave_halves(ffn_conv_w, d_ff, tn_ff)
    conv_b = _interleave_halves(ffn_conv_b[:, None, :], d_ff, tn_ff)

    gm_ln_g3, gm_ln_b3 = gm_ln_g[:, None, :], gm_ln_b[:, None, :]
    gm_b_s4 = gm_b_s[:, :, :, None]
    q_g3, kv_g3 = mla_q_norm_g[:, None, :], mla_kv_norm_g[:, None, :]
    mix_g3, mix_b3 = ln_mix_g[:, None, :], ln_mix_b[:, None, :]
    ffn_g3, ffn_b3 = ln_ffn_g[:, None, :], ln_ffn_b[:, None, :]

    xf = x.reshape(t, d)
    xb = xf
    t0, t1, t2 = _rope_tables(seq)

    for i in range(depth):
        slot = i // 2
        if i % 2 == 0:
            z = _gmlp_in(xb, gm_w_in_b, slot, tm_in, tn_in)
            xf, xb = _sgu_out(z, gm_ln_g3, gm_ln_b3, gm_w_s, gm_b_s4, gm_w_out_b, xf,
                              mix_g3, mix_b3, slot, i, alpha, tm)
        else:
            w_in = jnp.pad(mla_w_in[slot], ((0, 0), (0, V7X_LANES - ROPE_DIM))).astype(BF16)
            w_q = _pad_heads(mla_w_q_b[slot], NOPE_DIM + ROPE_DIM).astype(BF16)
            q, kc, v = _mla_proj(xb, w_in, q_g3, kv_g3, w_q, mla_w_kv_b, t0, t1, t2, slot, seq, tm)
            o = _attention(q, kc, v, batch, seq, tq)
            xf, xb = _attn_out(o, mla_w_out_b, xf, mix_g3, mix_b3, slot, i, alpha, tm)

        xf, xb = _conv_ffn(xb, ffn_w_up_b, conv_w, conv_b, ffn_w_down_b, xf, ffn_g3, ffn_b3,
                           i, alpha, seq, tm, tn_ff)

    return xf.reshape(batch, seq, d)
```

```python
import functools

import jax
import jax.numpy as jnp
from jax import lax
from jax.experimental import pallas as pl
from jax.experimental.pallas import tpu as pltpu

CHUNK = 64
GM_BLOCK = 128
GM_GROUPS = 8
MLA_HEADS = 16
Q_RANK = 512
KV_RANK = 512
NOPE_DIM = 128
ROPE_DIM = 64
V_DIM = 128
ROPE_THETA = 10000.0
SM_SCALE = (NOPE_DIM + ROPE_DIM) ** -0.5
LOG2_E = 1.4426950408889634
CONV_W = 3
LN_EPS = 1e-5
RMS_EPS = 1e-6

V7X_LANES = 128
V7X_SUBLANES = 8
V7X_VMEM_BYTES = 64 * 1024 * 1024
VMEM_LIMIT_BYTES = V7X_VMEM_BYTES - 6 * 1024 * 1024

QK_PAD = 2 * V7X_LANES
KV_WIDTH = NOPE_DIM + V_DIM
V_PAD = V_DIM + V7X_LANES
ATTN_HEADS_PER_STEP = 4
SGU_GROUPS_PER_STEP = 2

F32 = jnp.float32
BF16 = jnp.bfloat16


def _params(n_axes):
    return pltpu.CompilerParams(
        dimension_semantics=("arbitrary",) * n_axes,
        vmem_limit_bytes=VMEM_LIMIT_BYTES,
    )


def _gelu_tanh(x):
    c = (2.0 / jnp.pi) ** 0.5
    half_x = 0.5 * x
    return half_x + half_x * jnp.tanh(x * (c + (c * 0.044715) * (x * x)))


def _layer_norm(y, g, b):
    mu = jnp.mean(y, axis=-1, keepdims=True)
    d = y - mu
    var = jnp.mean(d * d, axis=-1, keepdims=True)
    return d * lax.rsqrt(var + LN_EPS) * g + b


def _rms_norm(y, g):
    return y * lax.rsqrt(jnp.mean(y * y, axis=-1, keepdims=True) + RMS_EPS) * g


def _store_residual_ln(x_ref, m, g_ref, b_ref, xo_ref, xob_ref, alpha):
    y = _layer_norm(alpha * x_ref[...] + m, g_ref[...], b_ref[...])
    xo_ref[...] = y
    xob_ref[...] = y.astype(BF16)


def _cast_kernel(w_ref, o_ref, *, segments, n_in_blocks, n_out_blocks):
    def copy():
        for dst, src, width in segments:
            if src is None:
                o_ref[:, dst:dst + width] = jnp.zeros((o_ref.shape[0], width), o_ref.dtype)
            else:
                o_ref[:, dst:dst + width] = w_ref[:, src:src + width].astype(o_ref.dtype)

    if n_in_blocks == n_out_blocks:
        copy()
    else:
        r = pl.program_id(1)
        pl.when(r < n_in_blocks)(copy)

        @pl.when(r >= n_in_blocks)
        def _():
            o_ref[...] = jnp.zeros_like(o_ref)


def _cast_weights(w, tr, segments=None, n_out=None, k_out=None):
    n_layers, k, n = w.shape
    n_out = n if n_out is None else n_out
    k_out = k if k_out is None else k_out
    segments = ((0, 0, n),) if segments is None else tuple(segments)
    assert k % tr == 0 and k_out % tr == 0
    n_in_blocks, n_out_blocks = k // tr, k_out // tr
    kern = functools.partial(_cast_kernel, segments=segments, n_in_blocks=n_in_blocks,
                             n_out_blocks=n_out_blocks)
    return pl.pallas_call(
        kern,
        grid=(n_layers, n_out_blocks),
        in_specs=[pl.BlockSpec((None, tr, n), lambda l, r: (l, jnp.minimum(r, n_in_blocks - 1), 0))],
        out_specs=pl.BlockSpec((None, tr, n_out), lambda l, r: (l, r, 0)),
        out_shape=jax.ShapeDtypeStruct((n_layers, k_out, n_out), BF16),
        compiler_params=_params(2),
        name="cast_weights",
    )(w)


def _gmlp_in_kernel(x_ref, w_ref, z_ref):
    h = jnp.dot(x_ref[...].astype(BF16), w_ref[...], preferred_element_type=F32)
    z_ref[...] = _gelu_tanh(h).astype(z_ref.dtype)


def _gmlp_in(xb, w_in, slot, tm, tn):
    t, d = xb.shape
    n = w_in.shape[2]
    return pl.pallas_call(
        _gmlp_in_kernel,
        grid=(t // tm, n // tn),
        in_specs=[
            pl.BlockSpec((tm, d), lambda i, j: (i, 0)),
            pl.BlockSpec((None, d, tn), lambda i, j: (slot, 0, j)),
        ],
        out_specs=pl.BlockSpec((tm, tn), lambda i, j: (i, j)),
        out_shape=jax.ShapeDtypeStruct((t, n), BF16),
        compiler_params=_params(2),
        name="gmlp_in",
    )(xb, w_in)


def _sgu_out_kernel(u_ref, v_ref, lng_ref, lnb_ref, ws_ref, bs_ref, wo_ref, x_ref, g_ref, b_ref,
                    xo_ref, xob_ref, p_ref, mu_ref, rs_ref, *, alpha, gd):
    grp = pl.program_id(1)
    acc_ref = xo_ref
    tm = u_ref.shape[0]

    @pl.when(grp == 0)
    def _():
        v = v_ref[...].astype(F32)
        mu = jnp.mean(v, axis=-1, keepdims=True)
        d = v - mu
        mu_ref[...] = mu
        rs_ref[...] = lax.rsqrt(jnp.mean(d * d, axis=-1, keepdims=True) + LN_EPS)
        acc_ref[...] = jnp.zeros_like(acc_ref)

    rows = lax.broadcasted_iota(jnp.int32, (GM_BLOCK, GM_BLOCK), 0) // CHUNK
    cols = lax.broadcasted_iota(jnp.int32, (GM_BLOCK, GM_BLOCK), 1) // CHUNK

    for gi in range(SGU_GROUPS_PER_STEP):
        off = pl.multiple_of((grp * SGU_GROUPS_PER_STEP + gi) * gd, V7X_LANES)
        cs = slice(gi * gd, (gi + 1) * gd)
        vg = v_ref[:, pl.ds(off, gd)].astype(F32)
        vn = ((vg - mu_ref[...]) * rs_ref[...] * lng_ref[:, cs] + lnb_ref[:, cs]).astype(BF16)
        wm = jnp.where(cols <= rows, ws_ref[gi], 0.0).astype(BF16)
        bs = bs_ref[gi]
        for r in range(tm // GM_BLOCK):
            sl = slice(r * GM_BLOCK, (r + 1) * GM_BLOCK)
            s = jnp.dot(wm, vn[sl, :], preferred_element_type=F32) + bs
            p_ref[sl, cs] = (u_ref[sl, cs].astype(F32) * s).astype(BF16)
    acc_ref[...] += jnp.dot(p_ref[...], wo_ref[...], preferred_element_type=F32)

    @pl.when(grp == pl.num_programs(1) - 1)
    def _():
        _store_residual_ln(x_ref, acc_ref[...], g_ref, b_ref, xo_ref, xob_ref, alpha)


def _sgu_out(z, ln_g, ln_b, w_s, b_s, w_out, x, g, b, slot, layer, alpha, tm):
    t, two_gh = z.shape
    gh = two_gh // 2
    gd = gh // GM_GROUPS
    d = x.shape[1]
    gps = SGU_GROUPS_PER_STEP
    kern = functools.partial(_sgu_out_kernel, alpha=alpha, gd=gd)
    return pl.pallas_call(
        kern,
        grid=(t // tm, GM_GROUPS // gps),
        in_specs=[
            pl.BlockSpec((tm, gps * gd), lambda i, k: (i, k)),
            pl.BlockSpec((tm, gh), lambda i, k: (i, 1)),
            pl.BlockSpec((None, 1, gps * gd), lambda i, k: (slot, 0, k)),
            pl.BlockSpec((None, 1, gps * gd), lambda i, k: (slot, 0, k)),
            pl.BlockSpec((None, gps, GM_BLOCK, GM_BLOCK), lambda i, k: (slot, k, 0, 0)),
            pl.BlockSpec((None, gps, GM_BLOCK, 1), lambda i, k: (slot, k, 0, 0)),
            pl.BlockSpec((None, gps * gd, d), lambda i, k: (slot, k, 0)),
            pl.BlockSpec((tm, d), lambda i, k: (i, 0)),
            pl.BlockSpec((None, 1, d), lambda i, k: (layer, 0, 0)),
            pl.BlockSpec((None, 1, d), lambda i, k: (layer, 0, 0)),
        ],
        out_specs=[
            pl.BlockSpec((tm, d), lambda i, k: (i, 0)),
            pl.BlockSpec((tm, d), lambda i, k: (i, 0)),
        ],
        out_shape=[jax.ShapeDtypeStruct((t, d), F32), jax.ShapeDtypeStruct((t, d), BF16)],
        scratch_shapes=[
            pltpu.VMEM((tm, gps * gd), BF16),
            pltpu.VMEM((tm, 1), F32),
            pltpu.VMEM((tm, 1), F32),
        ],
        compiler_params=_params(2),
        name="sgu_out",
    )(z, z, ln_g, ln_b, w_s, b_s, w_out, x, g, b)


def _ffn_kernel(xb_ref, wu_ref, cw_ref, cb_ref, wd_ref, x_ref, g_ref, b_ref,
                xo_ref, xob_ref, carry_ref, *, alpha, tiles_per_seq, tn_last):
    i = pl.program_id(0)
    j = pl.program_id(1)
    last = pl.num_programs(1) - 1

    @pl.when(j == 0)
    def _():
        xo_ref[...] = jnp.zeros_like(xo_ref)

    tile = functools.partial(_ffn_tile, xb_ref, wu_ref, cw_ref, cb_ref, wd_ref, xo_ref, carry_ref,
                             first=(i % tiles_per_seq) == 0, j=j)
    pl.when(j < last)(functools.partial(tile, tn=wd_ref.shape[0]))

    @pl.when(j == last)
    def _():
        tile(tn=tn_last)
        _store_residual_ln(x_ref, xo_ref[...], g_ref, b_ref, xo_ref, xob_ref, alpha)


def _ffn_tile(xb_ref, wu_ref, cw_ref, cb_ref, wd_ref, acc_ref, carry_ref, *, first, j, tn):
    tm = xb_ref.shape[0]
    w = 2 * tn
    h = jnp.dot(xb_ref[...], wu_ref[:, :w], preferred_element_type=F32)

    prev = jnp.where(first, 0.0, carry_ref[j, :, :w])
    carry_ref[j, :, :w] = h[tm - V7X_SUBLANES:, :]

    head_rows = lax.broadcasted_iota(jnp.int32, (V7X_SUBLANES, w), 0)
    cw = cw_ref[:, :w]
    out = cw[2:3, :] * h + cb_ref[:, :w]
    for shift in (1, 2):
        hs = pltpu.roll(h, shift, 0)
        head = jnp.where(head_rows < shift, pltpu.roll(prev, shift, 0), hs[:V7X_SUBLANES, :])
        hs = jnp.concatenate([head, hs[V7X_SUBLANES:, :]], axis=0)
        out = out + cw[2 - shift:3 - shift, :] * hs

    a = out[:, :tn]
    gt = out[:, tn:]
    act = (gt * (1.0 / (1.0 + jnp.exp2(gt * (-LOG2_E)))) * a).astype(BF16)
    acc_ref[...] += jnp.dot(act, wd_ref[:tn, :], preferred_element_type=F32)


def _conv_ffn(xb, w_up, conv_w, conv_b, w_down, x, g, b, layer, alpha, seq, tm, tn, tn_last):
    t, d = x.shape
    n_ff = w_down.shape[1] // tn
    kern = functools.partial(_ffn_kernel, alpha=alpha, tiles_per_seq=seq // tm, tn_last=tn_last)
    return pl.pallas_call(
        kern,
        grid=(t // tm, n_ff),
        in_specs=[
            pl.BlockSpec((tm, d), lambda i, j: (i, 0)),
            pl.BlockSpec((None, d, 2 * tn), lambda i, j: (layer, 0, j)),
            pl.BlockSpec((None, CONV_W, 2 * tn), lambda i, j: (layer, 0, j)),
            pl.BlockSpec((None, 1, 2 * tn), lambda i, j: (layer, 0, j)),
            pl.BlockSpec((None, tn, d), lambda i, j: (layer, j, 0)),
            pl.BlockSpec((tm, d), lambda i, j: (i, 0)),
            pl.BlockSpec((None, 1, d), lambda i, j: (layer, 0, 0)),
            pl.BlockSpec((None, 1, d), lambda i, j: (layer, 0, 0)),
        ],
        out_specs=[
            pl.BlockSpec((tm, d), lambda i, j: (i, 0)),
            pl.BlockSpec((tm, d), lambda i, j: (i, 0)),
        ],
        out_shape=[jax.ShapeDtypeStruct((t, d), F32), jax.ShapeDtypeStruct((t, d), BF16)],
        scratch_shapes=[pltpu.VMEM((n_ff, V7X_SUBLANES, 2 * tn), F32)],
        compiler_params=_params(2),
        name="conv_ffn",
    )(xb, w_up, conv_w, conv_b, w_down, x, g, b)


def _rope_pad(x, t0, t1, t2):
    half = ROPE_DIM // 2
    return x * t0 + pltpu.roll(x, half, 1) * t1 + pltpu.roll(x, V7X_LANES - half, 1) * t2


def _mla_proj_kernel(xb_ref, win_ref, qg_ref, kvg_ref, wq_ref, wkv_ref, t0_ref, t1_ref, t2_ref,
                     q_ref, kc_ref, v_ref):
    h = jnp.dot(xb_ref[...], win_ref[...], preferred_element_type=F32)
    cq = _rms_norm(h[:, :Q_RANK], qg_ref[...]).astype(BF16)
    ckv = _rms_norm(h[:, Q_RANK:Q_RANK + KV_RANK], kvg_ref[...]).astype(BF16)
    t0, t1, t2 = t0_ref[...], t1_ref[...], t2_ref[...]
    kr = _rope_pad(h[:, Q_RANK + KV_RANK:], t0, t1, t2).astype(BF16)

    q = jnp.dot(cq, wq_ref[...], preferred_element_type=F32)
    kv = jnp.dot(ckv, wkv_ref[...], preferred_element_type=F32)
    ones_col = (lax.broadcasted_iota(jnp.int32, (xb_ref.shape[0], V7X_LANES), 1) == 0).astype(BF16)
    for hd in range(MLA_HEADS):
        qb = hd * QK_PAD
        kb = hd * KV_WIDTH
        q_ref[:, qb:qb + NOPE_DIM] = q[:, qb:qb + NOPE_DIM].astype(BF16)
        q_ref[:, qb + NOPE_DIM:qb + QK_PAD] = _rope_pad(
            q[:, qb + NOPE_DIM:qb + QK_PAD], t0, t1, t2).astype(BF16)
        kc_ref[:, qb:qb + NOPE_DIM] = kv[:, kb:kb + NOPE_DIM].astype(BF16)
        kc_ref[:, qb + NOPE_DIM:qb + QK_PAD] = kr
        v_ref[:, hd * V_PAD:hd * V_PAD + V_DIM] = kv[:, kb + NOPE_DIM:kb + KV_WIDTH].astype(BF16)
        v_ref[:, hd * V_PAD + V_DIM:(hd + 1) * V_PAD] = ones_col


def _mla_proj(xb, w_in, q_g, kv_g, w_q, w_kv, t0, t1, t2, slot, seq, tm):
    t, d = xb.shape
    n_in = w_in.shape[1]
    tiles_per_seq = seq // tm
    hq = MLA_HEADS * QK_PAD
    rope_spec = pl.BlockSpec((tm, V7X_LANES), lambda i: (i % tiles_per_seq, 0))
    return pl.pallas_call(
        _mla_proj_kernel,
        grid=(t // tm,),
        in_specs=[
            pl.BlockSpec((tm, d), lambda i: (i, 0)),
            pl.BlockSpec((d, n_in), lambda i: (0, 0)),
            pl.BlockSpec((None, 1, Q_RANK), lambda i: (slot, 0, 0)),
            pl.BlockSpec((None, 1, KV_RANK), lambda i: (slot, 0, 0)),
            pl.BlockSpec((Q_RANK, hq), lambda i: (0, 0)),
            pl.BlockSpec((None, KV_RANK, MLA_HEADS * KV_WIDTH), lambda i: (slot, 0, 0)),
            rope_spec, rope_spec, rope_spec,
        ],
        out_specs=[
            pl.BlockSpec((tm, hq), lambda i: (i, 0)),
            pl.BlockSpec((tm, hq), lambda i: (i, 0)),
            pl.BlockSpec((tm, MLA_HEADS * V_PAD), lambda i: (i, 0)),
        ],
        out_shape=[
            jax.ShapeDtypeStruct((t, hq), BF16),
            jax.ShapeDtypeStruct((t, hq), BF16),
            jax.ShapeDtypeStruct((t, MLA_HEADS * V_PAD), BF16),
        ],
        compiler_params=_params(1),
        name="mla_proj",
    )(xb, w_in, q_g, kv_g, w_q, w_kv, t0, t1, t2)


def _attn_kernel(q_ref, k_ref, v_ref, o_ref):
    qi = pl.program_id(2)
    tq = q_ref.shape[0]
    c = SM_SCALE * LOG2_E
    qs = [q_ref[:, hd * QK_PAD:(hd + 1) * QK_PAD] for hd in range(ATTN_HEADS_PER_STEP)]

    def step(off, tk, carry, diagonal):
        new = []
        for hd in range(ATTN_HEADS_PER_STEP):
            m, acc = carry[hd]
            k = k_ref[pl.ds(off, tk), hd * QK_PAD:(hd + 1) * QK_PAD]
            v = v_ref[pl.ds(off, tk), hd * V_PAD:(hd + 1) * V_PAD]
            s = lax.dot_general(qs[hd], k, (((1,), (1,)), ((), ())), preferred_element_type=F32)
            if diagonal:
                rows = lax.broadcasted_iota(jnp.int32, (tq, tk), 0) // CHUNK
                cols = lax.broadcasted_iota(jnp.int32, (tq, tk), 1) // CHUNK
                s = jnp.where(cols <= rows, s, -jnp.inf)
            m_new = jnp.maximum(m, jnp.max(s, axis=-1, keepdims=True))
            a = jnp.exp2((m - m_new) * c)
            p = jnp.exp2((s - m_new) * c)
            acc = a * acc + jnp.dot(p.astype(BF16), v, preferred_element_type=F32)
            new.append((m_new, acc))
        return tuple(new)

    init = tuple(
        (jnp.full((tq, 1), -jnp.inf, F32), jnp.zeros((tq, V_PAD), F32))
        for _ in range(ATTN_HEADS_PER_STEP))
    carry = lax.fori_loop(
        0, qi // 2, lambda t, cr: step(pl.multiple_of(t * (2 * tq), 2 * tq), 2 * tq, cr, False), init)
    carry = lax.fori_loop(
        0, qi % 2, lambda t, cr: step(pl.multiple_of((qi - 1) * tq, tq), tq, cr, False), carry)
    carry = step(pl.multiple_of(qi * tq, tq), tq, carry, True)
    for hd in range(ATTN_HEADS_PER_STEP):
        _, acc = carry[hd]
        o_ref[:, hd * V_DIM:(hd + 1) * V_DIM] = (
            acc[:, :V_DIM] / acc[:, V_DIM:V_DIM + 1]).astype(o_ref.dtype)


def _attention(q, kc, v, batch, seq, tq):
    t = q.shape[0]
    nq = seq // tq
    hps = ATTN_HEADS_PER_STEP
    return pl.pallas_call(
        _attn_kernel,
        grid=(batch, MLA_HEADS // hps, nq),
        in_specs=[
            pl.BlockSpec((tq, hps * QK_PAD), lambda b, h, i: (b * nq + i, h)),
            pl.BlockSpec((seq, hps * QK_PAD), lambda b, h, i: (b, h)),
            pl.BlockSpec((seq, hps * V_PAD), lambda b, h, i: (b, h)),
        ],
        out_specs=pl.BlockSpec((tq, hps * V_DIM), lambda b, h, i: (b * nq + i, h)),
        out_shape=jax.ShapeDtypeStruct((t, MLA_HEADS * V_DIM), BF16),
        compiler_params=_params(3),
        name="attention",
    )(q, kc, v)


def _attn_out_kernel(o_ref, w_ref, x_ref, g_ref, b_ref, xo_ref, xob_ref, *, alpha):
    m = jnp.dot(o_ref[...], w_ref[...], preferred_element_type=F32)
    _store_residual_ln(x_ref, m, g_ref, b_ref, xo_ref, xob_ref, alpha)


def _attn_out(o, w_out, x, g, b, slot, layer, alpha, tm):
    t, d = x.shape
    k = o.shape[1]
    kern = functools.partial(_attn_out_kernel, alpha=alpha)
    return pl.pallas_call(
        kern,
        grid=(t // tm,),
        in_specs=[
            pl.BlockSpec((tm, k), lambda i: (i, 0)),
            pl.BlockSpec((None, k, d), lambda i: (slot, 0, 0)),
            pl.BlockSpec((tm, d), lambda i: (i, 0)),
            pl.BlockSpec((None, 1, d), lambda i: (layer, 0, 0)),
            pl.BlockSpec((None, 1, d), lambda i: (layer, 0, 0)),
        ],
        out_specs=[
            pl.BlockSpec((tm, d), lambda i: (i, 0)),
            pl.BlockSpec((tm, d), lambda i: (i, 0)),
        ],
        out_shape=[jax.ShapeDtypeStruct((t, d), F32), jax.ShapeDtypeStruct((t, d), BF16)],
        compiler_params=_params(1),
        name="attn_out",
    )(o, w_out, x, g, b)


def _round_up(n, m):
    return (n + m - 1) // m * m


def _ffn_tiles(d_ff, tn, tn_last):
    return max(0, -(-(d_ff - tn_last) // tn)) + 1


def _ffn_segments(d_ff, tn, tn_last):
    segs = []
    n = _ffn_tiles(d_ff, tn, tn_last)
    for j in range(n):
        half_w = tn if j < n - 1 else tn_last
        valid = min(max(d_ff - j * tn, 0), half_w)
        for half in range(2):
            dst = 2 * j * tn + half * half_w
            if valid:
                segs.append((dst, half * d_ff + j * tn, valid))
            if valid < half_w:
                segs.append((dst + valid, None, half_w - valid))
        if half_w < tn:
            segs.append((2 * j * tn + 2 * half_w, None, 2 * (tn - half_w)))
    return segs


def _interleave_halves(w, d_ff, tn, tn_last):
    parts = []
    for dst, src, width in _ffn_segments(d_ff, tn, tn_last):
        parts.append(jnp.zeros(w.shape[:-1] + (width,), w.dtype) if src is None
                     else w[..., src:src + width])
    return jnp.concatenate(parts, axis=-1)


def _rope_tables(seq):
    half = ROPE_DIM // 2
    inv_freq = ROPE_THETA ** (-jnp.arange(half, dtype=F32) / half)
    ang = jnp.arange(seq, dtype=F32)[:, None] * inv_freq[None, :]
    cos, sin = jnp.cos(ang), jnp.sin(ang)
    z = jnp.zeros_like(cos)
    zz = jnp.zeros((seq, V7X_LANES - ROPE_DIM), F32)
    t0 = jnp.concatenate([cos, cos, zz], axis=1)
    t1 = jnp.concatenate([z, sin, zz], axis=1)
    t2 = jnp.concatenate([-sin, z, zz], axis=1)
    return t0, t1, t2


def _pad_heads(w, width):
    k = w.shape[0]
    w = w.reshape(k, MLA_HEADS, width)
    w = jnp.pad(w, ((0, 0), (0, 0), (0, QK_PAD - width)))
    return w.reshape(k, MLA_HEADS * QK_PAD)


def kernel(x, gm_w_in, gm_ln_g, gm_ln_b, gm_w_s, gm_b_s, gm_w_out, mla_w_in, mla_q_norm_g, mla_kv_norm_g, mla_w_q_b, mla_w_kv_b, mla_w_out, ffn_w_up, ffn_conv_w, ffn_conv_b, ffn_w_down, ln_mix_g, ln_mix_b, ln_ffn_g, ln_ffn_b):
    batch, seq, d = x.shape
    t = batch * seq
    depth = ffn_w_up.shape[0]
    d_ff = ffn_w_down.shape[1]
    alpha = float((2 * depth) ** 0.25)

    tm = min(512, seq)
    tm_in = min(1024, seq)
    tn_in = 2048
    tn_ff = 1024
    tn_ff_last = 512
    tq = min(512, seq)
    assert seq % tm == 0 and seq % tm_in == 0 and seq % tq == 0 and tm % GM_BLOCK == 0
    assert (2 * gm_w_out.shape[1]) % tn_in == 0 and tq % CHUNK == 0
    ffp = tn_ff * _ffn_tiles(d_ff, tn_ff, tn_ff_last)

    gm_w_in_b = _cast_weights(gm_w_in, tr=256)
    gm_w_out_b = _cast_weights(gm_w_out, tr=1024)
    mla_w_kv_b = _cast_weights(mla_w_kv_b, tr=KV_RANK)
    mla_w_out_b = _cast_weights(mla_w_out, tr=1024)
    ffn_w_up_b = _cast_weights(ffn_w_up, tr=256, segments=_ffn_segments(d_ff, tn_ff, tn_ff_last),
                               n_out=2 * ffp)
    ffn_w_down_b = _cast_weights(ffn_w_down, tr=V7X_LANES, k_out=ffp)
    conv_w = _interleave_halves(ffn_conv_w, d_ff, tn_ff, tn_ff_last)
    conv_b = _interleave_halves(ffn_conv_b[:, None, :], d_ff, tn_ff, tn_ff_last)

    gm_ln_g3, gm_ln_b3 = gm_ln_g[:, None, :], gm_ln_b[:, None, :]
    gm_b_s4 = gm_b_s[:, :, :, None]
    q_g3, kv_g3 = mla_q_norm_g[:, None, :], mla_kv_norm_g[:, None, :]
    mix_g3, mix_b3 = ln_mix_g[:, None, :], ln_mix_b[:, None, :]
    ffn_g3, ffn_b3 = ln_ffn_g[:, None, :], ln_ffn_b[:, None, :]

    xf = x.reshape(t, d)
    xb = xf
    t0, t1, t2 = _rope_tables(seq)

    for i in range(depth):
        slot = i // 2
        if i % 2 == 0:
            z = _gmlp_in(xb, gm_w_in_b, slot, tm_in, tn_in)
            xf, xb = _sgu_out(z, gm_ln_g3, gm_ln_b3, gm_w_s, gm_b_s4, gm_w_out_b, xf,
                              mix_g3, mix_b3, slot, i, alpha, tm)
        else:
            w_in = jnp.pad(mla_w_in[slot], ((0, 0), (0, V7X_LANES - ROPE_DIM))).astype(BF16)
            w_q = _pad_heads(mla_w_q_b[slot], NOPE_DIM + ROPE_DIM).astype(BF16)
            q, kc, v = _mla_proj(xb, w_in, q_g3, kv_g3, w_q, mla_w_kv_b, t0, t1, t2, slot, seq, tm)
            o = _attention(q, kc, v, batch, seq, tq)
            xf, xb = _attn_out(o, mla_w_out_b, xf, mix_g3, mix_b3, slot, i, alpha, tm)

        xf, xb = _conv_ffn(xb, ffn_w_up_b, conv_w, conv_b, ffn_w_down_b, xf, ffn_g3, ffn_b3,
                           i, alpha, seq, tm, tn_ff, tn_ff_last)

    return xf.reshape(batch, seq, d)
```

```python
import functools

import jax
import jax.numpy as jnp
from jax import lax
from jax.experimental import pallas as pl
from jax.experimental.pallas import tpu as pltpu

CHUNK = 64
GM_BLOCK = 128
GM_GROUPS = 8
MLA_HEADS = 16
Q_RANK = 512
KV_RANK = 512
NOPE_DIM = 128
ROPE_DIM = 64
V_DIM = 128
ROPE_THETA = 10000.0
SM_SCALE = (NOPE_DIM + ROPE_DIM) ** -0.5
LOG2_E = 1.4426950408889634
CONV_W = 3
LN_EPS = 1e-5
RMS_EPS = 1e-6

V7X_LANES = 128
V7X_SUBLANES = 8
V7X_VMEM_BYTES = 64 * 1024 * 1024
VMEM_LIMIT_BYTES = V7X_VMEM_BYTES - 6 * 1024 * 1024

QK_PAD = 2 * V7X_LANES
KV_WIDTH = NOPE_DIM + V_DIM
V_PAD = V_DIM + V7X_LANES
ATTN_HEADS_PER_STEP = 4
SGU_GROUPS_PER_STEP = 2

F32 = jnp.float32
BF16 = jnp.bfloat16


def _params(n_axes):
    return pltpu.CompilerParams(
        dimension_semantics=("arbitrary",) * n_axes,
        vmem_limit_bytes=VMEM_LIMIT_BYTES,
    )


def _gelu_tanh(x):
    c = (2.0 / jnp.pi) ** 0.5
    half_x = 0.5 * x
    return half_x + half_x * jnp.tanh(x * (c + (c * 0.044715) * (x * x)))


def _layer_norm(y, g, b):
    mu = jnp.mean(y, axis=-1, keepdims=True)
    d = y - mu
    var = jnp.mean(d * d, axis=-1, keepdims=True)
    return d * lax.rsqrt(var + LN_EPS) * g + b


def _rms_norm(y, g):
    return y * lax.rsqrt(jnp.mean(y * y, axis=-1, keepdims=True) + RMS_EPS) * g


def _store_residual_ln(x_ref, m, g_ref, b_ref, xo_ref, xob_ref, alpha):
    y = _layer_norm(alpha * x_ref[...] + m, g_ref[...], b_ref[...])
    xo_ref[...] = y
    xob_ref[...] = y.astype(BF16)


def _cast_kernel(w_ref, o_ref, *, segments, n_in_blocks, n_out_blocks):
    def copy():
        for dst, src, width in segments:
            if src is None:
                o_ref[:, dst:dst + width] = jnp.zeros((o_ref.shape[0], width), o_ref.dtype)
            else:
                o_ref[:, dst:dst + width] = w_ref[:, src:src + width].astype(o_ref.dtype)

    if n_in_blocks == n_out_blocks:
        copy()
    else:
        r = pl.program_id(1)
        pl.when(r < n_in_blocks)(copy)

        @pl.when(r >= n_in_blocks)
        def _():
            o_ref[...] = jnp.zeros_like(o_ref)


def _cast_weights(w, tr, segments=None, n_out=None, k_out=None):
    n_layers, k, n = w.shape
    n_out = n if n_out is None else n_out
    k_out = k if k_out is None else k_out
    segments = ((0, 0, n),) if segments is None else tuple(segments)
    assert k % tr == 0 and k_out % tr == 0
    n_in_blocks, n_out_blocks = k // tr, k_out // tr
    kern = functools.partial(_cast_kernel, segments=segments, n_in_blocks=n_in_blocks,
                             n_out_blocks=n_out_blocks)
    return pl.pallas_call(
        kern,
        grid=(n_layers, n_out_blocks),
        in_specs=[pl.BlockSpec((None, tr, n), lambda l, r: (l, jnp.minimum(r, n_in_blocks - 1), 0))],
        out_specs=pl.BlockSpec((None, tr, n_out), lambda l, r: (l, r, 0)),
        out_shape=jax.ShapeDtypeStruct((n_layers, k_out, n_out), BF16),
        compiler_params=_params(2),
        name="cast_weights",
    )(w)


def _gmlp_in_kernel(x_ref, w_ref, z_ref):
    h = jnp.dot(x_ref[...].astype(BF16), w_ref[...], preferred_element_type=F32)
    z_ref[...] = _gelu_tanh(h).astype(z_ref.dtype)


def _gmlp_in(xb, w_in, slot, tm, tn):
    t, d = xb.shape
    n = w_in.shape[2]
    return pl.pallas_call(
        _gmlp_in_kernel,
        grid=(t // tm, n // tn),
        in_specs=[
            pl.BlockSpec((tm, d), lambda i, j: (i, 0)),
            pl.BlockSpec((None, d, tn), lambda i, j: (slot, 0, j)),
        ],
        out_specs=pl.BlockSpec((tm, tn), lambda i, j: (i, j)),
        out_shape=jax.ShapeDtypeStruct((t, n), BF16),
        compiler_params=_params(2),
        name="gmlp_in",
    )(xb, w_in)


def _sgu_out_kernel(u_ref, v_ref, lng_ref, lnb_ref, ws_ref, bs_ref, wo_ref, x_ref, g_ref, b_ref,
                    xo_ref, xob_ref, p_ref, mu_ref, rs_ref, *, alpha, gd):
    grp = pl.program_id(1)
    acc_ref = xo_ref
    tm = u_ref.shape[0]

    @pl.when(grp == 0)
    def _():
        v = v_ref[...].astype(F32)
        mu = jnp.mean(v, axis=-1, keepdims=True)
        d = v - mu
        mu_ref[...] = mu
        rs_ref[...] = lax.rsqrt(jnp.mean(d * d, axis=-1, keepdims=True) + LN_EPS)
        acc_ref[...] = jnp.zeros_like(acc_ref)

    rows = lax.broadcasted_iota(jnp.int32, (GM_BLOCK, GM_BLOCK), 0) // CHUNK
    cols = lax.broadcasted_iota(jnp.int32, (GM_BLOCK, GM_BLOCK), 1) // CHUNK

    for gi in range(SGU_GROUPS_PER_STEP):
        off = pl.multiple_of((grp * SGU_GROUPS_PER_STEP + gi) * gd, V7X_LANES)
        cs = slice(gi * gd, (gi + 1) * gd)
        vg = v_ref[:, pl.ds(off, gd)].astype(F32)
        vn = ((vg - mu_ref[...]) * rs_ref[...] * lng_ref[:, cs] + lnb_ref[:, cs]).astype(BF16)
        wm = jnp.where(cols <= rows, ws_ref[gi], 0.0).astype(BF16)
        bs = bs_ref[gi]
        for r in range(tm // GM_BLOCK):
            sl = slice(r * GM_BLOCK, (r + 1) * GM_BLOCK)
            s = jnp.dot(wm, vn[sl, :], preferred_element_type=F32) + bs
            p_ref[sl, cs] = (u_ref[sl, cs].astype(F32) * s).astype(BF16)
    acc_ref[...] += jnp.dot(p_ref[...], wo_ref[...], preferred_element_type=F32)

    @pl.when(grp == pl.num_programs(1) - 1)
    def _():
        _store_residual_ln(x_ref, acc_ref[...], g_ref, b_ref, xo_ref, xob_ref, alpha)


def _sgu_out(z, ln_g, ln_b, w_s, b_s, w_out, x, g, b, slot, layer, alpha, tm):
    t, two_gh = z.shape
    gh = two_gh // 2
    gd = gh // GM_GROUPS
    d = x.shape[1]
    gps = SGU_GROUPS_PER_STEP
    kern = functools.partial(_sgu_out_kernel, alpha=alpha, gd=gd)
    return pl.pallas_call(
        kern,
        grid=(t // tm, GM_GROUPS // gps),
        in_specs=[
            pl.BlockSpec((tm, gps * gd), lambda i, k: (i, k)),
            pl.BlockSpec((tm, gh), lambda i, k: (i, 1)),
            pl.BlockSpec((None, 1, gps * gd), lambda i, k: (slot, 0, k)),
            pl.BlockSpec((None, 1, gps * gd), lambda i, k: (slot, 0, k)),
            pl.BlockSpec((None, gps, GM_BLOCK, GM_BLOCK), lambda i, k: (slot, k, 0, 0)),
            pl.BlockSpec((None, gps, GM_BLOCK, 1), lambda i, k: (slot, k, 0, 0)),
            pl.BlockSpec((None, gps * gd, d), lambda i, k: (slot, k, 0)),
            pl.BlockSpec((tm, d), lambda i, k: (i, 0)),
            pl.BlockSpec((None, 1, d), lambda i, k: (layer, 0, 0)),
            pl.BlockSpec((None, 1, d), lambda i, k: (layer, 0, 0)),
        ],
        out_specs=[
            pl.BlockSpec((tm, d), lambda i, k: (i, 0)),
            pl.BlockSpec((tm, d), lambda i, k: (i, 0)),
        ],
        out_shape=[jax.ShapeDtypeStruct((t, d), F32), jax.ShapeDtypeStruct((t, d), BF16)],
        scratch_shapes=[
            pltpu.VMEM((tm, gps * gd), BF16),
            pltpu.VMEM((tm, 1), F32),
            pltpu.VMEM((tm, 1), F32),
        ],
        compiler_params=_params(2),
        name="sgu_out",
    )(z, z, ln_g, ln_b, w_s, b_s, w_out, x, g, b)


def _ffn_kernel(xb_ref, wu_ref, cw_ref, cb_ref, wd_ref, x_ref, g_ref, b_ref,
                xo_ref, xob_ref, carry_ref, *, alpha, tiles_per_seq, tn_last):
    i = pl.program_id(0)
    j = pl.program_id(1)
    last = pl.num_programs(1) - 1

    tile = functools.partial(_ffn_tile, xb_ref, wu_ref, cw_ref, cb_ref, wd_ref, xo_ref, carry_ref,
                             first=(i % tiles_per_seq) == 0, j=j)
    pl.when(j == 0)(functools.partial(tile, tn=wd_ref.shape[0], assign=True))
    pl.when((j > 0) & (j < last))(functools.partial(tile, tn=wd_ref.shape[0], assign=False))

    @pl.when(j == last)
    def _():
        tile(tn=tn_last, assign=False)
        _store_residual_ln(x_ref, xo_ref[...], g_ref, b_ref, xo_ref, xob_ref, alpha)


def _ffn_tile(xb_ref, wu_ref, cw_ref, cb_ref, wd_ref, acc_ref, carry_ref, *, first, j, tn, assign):
    tm = xb_ref.shape[0]
    w = 2 * tn
    h = jnp.dot(xb_ref[...], wu_ref[:, :w], preferred_element_type=F32)

    prev = jnp.where(first, 0.0, carry_ref[j, :, :w])
    carry_ref[j, :, :w] = h[tm - V7X_SUBLANES:, :]

    head_rows = lax.broadcasted_iota(jnp.int32, (V7X_SUBLANES, w), 0)
    cw = cw_ref[:, :w]
    out = cw[2:3, :] * h + cb_ref[:, :w]
    for shift in (1, 2):
        hs = pltpu.roll(h, shift, 0)
        head = jnp.where(head_rows < shift, pltpu.roll(prev, shift, 0), hs[:V7X_SUBLANES, :])
        hs = jnp.concatenate([head, hs[V7X_SUBLANES:, :]], axis=0)
        out = out + cw[2 - shift:3 - shift, :] * hs

    a = out[:, :tn]
    gt = out[:, tn:]
    act = (gt * (1.0 / (1.0 + jnp.exp2(gt * (-LOG2_E)))) * a).astype(BF16)
    down = jnp.dot(act, wd_ref[:tn, :], preferred_element_type=F32)
    if assign:
        acc_ref[...] = down
    else:
        acc_ref[...] += down


def _conv_ffn(xb, w_up, conv_w, conv_b, w_down, x, g, b, layer, alpha, seq, tm, tn, tn_last):
    t, d = x.shape
    n_ff = w_down.shape[1] // tn
    assert n_ff >= 2
    kern = functools.partial(_ffn_kernel, alpha=alpha, tiles_per_seq=seq // tm, tn_last=tn_last)
    return pl.pallas_call(
        kern,
        grid=(t // tm, n_ff),
        in_specs=[
            pl.BlockSpec((tm, d), lambda i, j: (i, 0)),
            pl.BlockSpec((None, d, 2 * tn), lambda i, j: (layer, 0, j)),
            pl.BlockSpec((None, CONV_W, 2 * tn), lambda i, j: (layer, 0, j)),
            pl.BlockSpec((None, 1, 2 * tn), lambda i, j: (layer, 0, j)),
            pl.BlockSpec((None, tn, d), lambda i, j: (layer, j, 0)),
            pl.BlockSpec((tm, d), lambda i, j: (i, 0)),
            pl.BlockSpec((None, 1, d), lambda i, j: (layer, 0, 0)),
            pl.BlockSpec((None, 1, d), lambda i, j: (layer, 0, 0)),
        ],
        out_specs=[
            pl.BlockSpec((tm, d), lambda i, j: (i, 0)),
            pl.BlockSpec((tm, d), lambda i, j: (i, 0)),
        ],
        out_shape=[jax.ShapeDtypeStruct((t, d), F32), jax.ShapeDtypeStruct((t, d), BF16)],
        scratch_shapes=[pltpu.VMEM((n_ff, V7X_SUBLANES, 2 * tn), F32)],
        compiler_params=_params(2),
        name="conv_ffn",
    )(xb, w_up, conv_w, conv_b, w_down, x, g, b)


def _rope_pad(x, t0, t1, t2):
    half = ROPE_DIM // 2
    return x * t0 + pltpu.roll(x, half, 1) * t1 + pltpu.roll(x, V7X_LANES - half, 1) * t2


def _mla_proj_kernel(xb_ref, win_ref, qg_ref, kvg_ref, wq_ref, wkv_ref, t0_ref, t1_ref, t2_ref,
                     q_ref, kc_ref, v_ref):
    h = jnp.dot(xb_ref[...], win_ref[...], preferred_element_type=F32)
    cq = _rms_norm(h[:, :Q_RANK], qg_ref[...]).astype(BF16)
    ckv = _rms_norm(h[:, Q_RANK:Q_RANK + KV_RANK], kvg_ref[...]).astype(BF16)
    t0, t1, t2 = t0_ref[...], t1_ref[...], t2_ref[...]
    kr = _rope_pad(h[:, Q_RANK + KV_RANK:], t0, t1, t2).astype(BF16)

    q = jnp.dot(cq, wq_ref[...], preferred_element_type=F32)
    kv = jnp.dot(ckv, wkv_ref[...], preferred_element_type=F32)
    ones_col = (lax.broadcasted_iota(jnp.int32, (xb_ref.shape[0], V7X_LANES), 1) == 0).astype(BF16)
    for hd in range(MLA_HEADS):
        qb = hd * QK_PAD
        kb = hd * KV_WIDTH
        q_ref[:, qb:qb + NOPE_DIM] = q[:, qb:qb + NOPE_DIM].astype(BF16)
        q_ref[:, qb + NOPE_DIM:qb + QK_PAD] = _rope_pad(
            q[:, qb + NOPE_DIM:qb + QK_PAD], t0, t1, t2).astype(BF16)
        kc_ref[:, qb:qb + NOPE_DIM] = kv[:, kb:kb + NOPE_DIM].astype(BF16)
        kc_ref[:, qb + NOPE_DIM:qb + QK_PAD] = kr
        v_ref[:, hd * V_PAD:hd * V_PAD + V_DIM] = kv[:, kb + NOPE_DIM:kb + KV_WIDTH].astype(BF16)
        v_ref[:, hd * V_PAD + V_DIM:(hd + 1) * V_PAD] = ones_col


def _mla_proj(xb, w_in, q_g, kv_g, w_q, w_kv, t0, t1, t2, slot, seq, tm):
    t, d = xb.shape
    n_in = w_in.shape[1]
    tiles_per_seq = seq // tm
    hq = MLA_HEADS * QK_PAD
    rope_spec = pl.BlockSpec((tm, V7X_LANES), lambda i: (i % tiles_per_seq, 0))
    return pl.pallas_call(
        _mla_proj_kernel,
        grid=(t // tm,),
        in_specs=[
            pl.BlockSpec((tm, d), lambda i: (i, 0)),
            pl.BlockSpec((d, n_in), lambda i: (0, 0)),
            pl.BlockSpec((None, 1, Q_RANK), lambda i: (slot, 0, 0)),
            pl.BlockSpec((None, 1, KV_RANK), lambda i: (slot, 0, 0)),
            pl.BlockSpec((Q_RANK, hq), lambda i: (0, 0)),
            pl.BlockSpec((None, KV_RANK, MLA_HEADS * KV_WIDTH), lambda i: (slot, 0, 0)),
            rope_spec, rope_spec, rope_spec,
        ],
        out_specs=[
            pl.BlockSpec((tm, hq), lambda i: (i, 0)),
            pl.BlockSpec((tm, hq), lambda i: (i, 0)),
            pl.BlockSpec((tm, MLA_HEADS * V_PAD), lambda i: (i, 0)),
        ],
        out_shape=[
            jax.ShapeDtypeStruct((t, hq), BF16),
            jax.ShapeDtypeStruct((t, hq), BF16),
            jax.ShapeDtypeStruct((t, MLA_HEADS * V_PAD), BF16),
        ],
        compiler_params=_params(1),
        name="mla_proj",
    )(xb, w_in, q_g, kv_g, w_q, w_kv, t0, t1, t2)


def _attn_kernel(q_ref, k_ref, v_ref, o_ref):
    qi = pl.program_id(2)
    tq = q_ref.shape[0]
    c = SM_SCALE * LOG2_E
    qs = [q_ref[:, hd * QK_PAD:(hd + 1) * QK_PAD] for hd in range(ATTN_HEADS_PER_STEP)]

    def step(off, tk, carry, diagonal):
        new = []
        for hd in range(ATTN_HEADS_PER_STEP):
            m, acc = carry[hd]
            k = k_ref[pl.ds(off, tk), hd * QK_PAD:(hd + 1) * QK_PAD]
            v = v_ref[pl.ds(off, tk), hd * V_PAD:(hd + 1) * V_PAD]
            s = lax.dot_general(qs[hd], k, (((1,), (1,)), ((), ())), preferred_element_type=F32)
            if diagonal:
                rows = lax.broadcasted_iota(jnp.int32, (tq, tk), 0) // CHUNK
                cols = lax.broadcasted_iota(jnp.int32, (tq, tk), 1) // CHUNK
                s = jnp.where(cols <= rows, s, -jnp.inf)
            m_new = jnp.maximum(m, jnp.max(s, axis=-1, keepdims=True))
            a = jnp.exp2((m - m_new) * c)
            p = jnp.exp2((s - m_new) * c)
            acc = a * acc + jnp.dot(p.astype(BF16), v, preferred_element_type=F32)
            new.append((m_new, acc))
        return tuple(new)

    init = tuple(
        (jnp.full((tq, 1), -jnp.inf, F32), jnp.zeros((tq, V_PAD), F32))
        for _ in range(ATTN_HEADS_PER_STEP))
    carry = lax.fori_loop(
        0, qi // 2, lambda t, cr: step(pl.multiple_of(t * (2 * tq), 2 * tq), 2 * tq, cr, False), init)
    carry = lax.fori_loop(
        0, qi % 2, lambda t, cr: step(pl.multiple_of((qi - 1) * tq, tq), tq, cr, False), carry)
    carry = step(pl.multiple_of(qi * tq, tq), tq, carry, True)
    for hd in range(ATTN_HEADS_PER_STEP):
        _, acc = carry[hd]
        o_ref[:, hd * V_DIM:(hd + 1) * V_DIM] = (
            acc[:, :V_DIM] / acc[:, V_DIM:V_DIM + 1]).astype(o_ref.dtype)


def _attention(q, kc, v, batch, seq, tq):
    t = q.shape[0]
    nq = seq // tq
    hps = ATTN_HEADS_PER_STEP
    return pl.pallas_call(
        _attn_kernel,
        grid=(batch, MLA_HEADS // hps, nq),
        in_specs=[
            pl.BlockSpec((tq, hps * QK_PAD), lambda b, h, i: (b * nq + i, h)),
            pl.BlockSpec((seq, hps * QK_PAD), lambda b, h, i: (b, h)),
            pl.BlockSpec((seq, hps * V_PAD), lambda b, h, i: (b, h)),
        ],
        out_specs=pl.BlockSpec((tq, hps * V_DIM), lambda b, h, i: (b * nq + i, h)),
        out_shape=jax.ShapeDtypeStruct((t, MLA_HEADS * V_DIM), BF16),
        compiler_params=_params(3),
        name="attention",
    )(q, kc, v)


def _attn_out_kernel(o_ref, w_ref, x_ref, g_ref, b_ref, xo_ref, xob_ref, *, alpha):
    m = jnp.dot(o_ref[...], w_ref[...], preferred_element_type=F32)
    _store_residual_ln(x_ref, m, g_ref, b_ref, xo_ref, xob_ref, alpha)


def _attn_out(o, w_out, x, g, b, slot, layer, alpha, tm):
    t, d = x.shape
    k = o.shape[1]
    kern = functools.partial(_attn_out_kernel, alpha=alpha)
    return pl.pallas_call(
        kern,
        grid=(t // tm,),
        in_specs=[
            pl.BlockSpec((tm, k), lambda i: (i, 0)),
            pl.BlockSpec((None, k, d), lambda i: (slot, 0, 0)),
            pl.BlockSpec((tm, d), lambda i: (i, 0)),
            pl.BlockSpec((None, 1, d), lambda i: (layer, 0, 0)),
            pl.BlockSpec((None, 1, d), lambda i: (layer, 0, 0)),
        ],
        out_specs=[
            pl.BlockSpec((tm, d), lambda i: (i, 0)),
            pl.BlockSpec((tm, d), lambda i: (i, 0)),
        ],
        out_shape=[jax.ShapeDtypeStruct((t, d), F32), jax.ShapeDtypeStruct((t, d), BF16)],
        compiler_params=_params(1),
        name="attn_out",
    )(o, w_out, x, g, b)


def _round_up(n, m):
    return (n + m - 1) // m * m


def _ffn_tiles(d_ff, tn, tn_last):
    return max(0, -(-(d_ff - tn_last) // tn)) + 1


def _ffn_segments(d_ff, tn, tn_last):
    segs = []
    n = _ffn_tiles(d_ff, tn, tn_last)
    for j in range(n):
        half_w = tn if j < n - 1 else tn_last
        valid = min(max(d_ff - j * tn, 0), half_w)
        for half in range(2):
            dst = 2 * j * tn + half * half_w
            if valid:
                segs.append((dst, half * d_ff + j * tn, valid))
            if valid < half_w:
                segs.append((dst + valid, None, half_w - valid))
        if half_w < tn:
            segs.append((2 * j * tn + 2 * half_w, None, 2 * (tn - half_w)))
    return segs


def _interleave_halves(w, d_ff, tn, tn_last):
    parts = []
    for dst, src, width in _ffn_segments(d_ff, tn, tn_last):
        parts.append(jnp.zeros(w.shape[:-1] + (width,), w.dtype) if src is None
                     else w[..., src:src + width])
    return jnp.concatenate(parts, axis=-1)


def _rope_tables(seq):
    half = ROPE_DIM // 2
    inv_freq = ROPE_THETA ** (-jnp.arange(half, dtype=F32) / half)
    ang = jnp.arange(seq, dtype=F32)[:, None] * inv_freq[None, :]
    cos, sin = jnp.cos(ang), jnp.sin(ang)
    z = jnp.zeros_like(cos)
    zz = jnp.zeros((seq, V7X_LANES - ROPE_DIM), F32)
    t0 = jnp.concatenate([cos, cos, zz], axis=1)
    t1 = jnp.concatenate([z, sin, zz], axis=1)
    t2 = jnp.concatenate([-sin, z, zz], axis=1)
    return t0, t1, t2


def _pad_heads(w, width):
    k = w.shape[0]
    w = w.reshape(k, MLA_HEADS, width)
    w = jnp.pad(w, ((0, 0), (0, 0), (0, QK_PAD - width)))
    return w.reshape(k, MLA_HEADS * QK_PAD)


def kernel(x, gm_w_in, gm_ln_g, gm_ln_b, gm_w_s, gm_b_s, gm_w_out, mla_w_in, mla_q_norm_g, mla_kv_norm_g, mla_w_q_b, mla_w_kv_b, mla_w_out, ffn_w_up, ffn_conv_w, ffn_conv_b, ffn_w_down, ln_mix_g, ln_mix_b, ln_ffn_g, ln_ffn_b):
    batch, seq, d = x.shape
    t = batch * seq
    depth = ffn_w_up.shape[0]
    d_ff = ffn_w_down.shape[1]
    alpha = float((2 * depth) ** 0.25)

    tm = min(512, seq)
    tm_in = min(1024, seq)
    tn_in = 2048
    tn_ff = 1024
    tn_ff_last = 512
    tq = min(512, seq)
    assert seq % tm == 0 and seq % tm_in == 0 and seq % tq == 0 and tm % GM_BLOCK == 0
    assert (2 * gm_w_out.shape[1]) % tn_in == 0 and tq % CHUNK == 0
    ffp = tn_ff * _ffn_tiles(d_ff, tn_ff, tn_ff_last)

    gm_w_in_b = _cast_weights(gm_w_in, tr=256)
    gm_w_out_b = _cast_weights(gm_w_out, tr=1024)
    mla_w_kv_b = _cast_weights(mla_w_kv_b, tr=KV_RANK)
    mla_w_out_b = _cast_weights(mla_w_out, tr=1024)
    ffn_w_up_b = _cast_weights(ffn_w_up, tr=256, segments=_ffn_segments(d_ff, tn_ff, tn_ff_last),
                               n_out=2 * ffp)
    ffn_w_down_b = _cast_weights(ffn_w_down, tr=V7X_LANES, k_out=ffp)
    conv_w = _interleave_halves(ffn_conv_w, d_ff, tn_ff, tn_ff_last)
    conv_b = _interleave_halves(ffn_conv_b[:, None, :], d_ff, tn_ff, tn_ff_last)

    gm_ln_g3, gm_ln_b3 = gm_ln_g[:, None, :], gm_ln_b[:, None, :]
    gm_b_s4 = gm_b_s[:, :, :, None]
    q_g3, kv_g3 = mla_q_norm_g[:, None, :], mla_kv_norm_g[:, None, :]
    mix_g3, mix_b3 = ln_mix_g[:, None, :], ln_mix_b[:, None, :]
    ffn_g3, ffn_b3 = ln_ffn_g[:, None, :], ln_ffn_b[:, None, :]

    xf = x.reshape(t, d)
    xb = xf
    t0, t1, t2 = _rope_tables(seq)

    for i in range(depth):
        slot = i // 2
        if i % 2 == 0:
            z = _gmlp_in(xb, gm_w_in_b, slot, tm_in, tn_in)
            xf, xb = _sgu_out(z, gm_ln_g3, gm_ln_b3, gm_w_s, gm_b_s4, gm_w_out_b, xf,
                              mix_g3, mix_b3, slot, i, alpha, tm)
        else:
            w_in = jnp.pad(mla_w_in[slot], ((0, 0), (0, V7X_LANES - ROPE_DIM))).astype(BF16)
            w_q = _pad_heads(mla_w_q_b[slot], NOPE_DIM + ROPE_DIM).astype(BF16)
            q, kc, v = _mla_proj(xb, w_in, q_g3, kv_g3, w_q, mla_w_kv_b, t0, t1, t2, slot, seq, tm)
            o = _attention(q, kc, v, batch, seq, tq)
            xf, xb = _attn_out(o, mla_w_out_b, xf, mix_g3, mix_b3, slot, i, alpha, tm)

        xf, xb = _conv_ffn(xb, ffn_w_up_b, conv_w, conv_b, ffn_w_down_b, xf, ffn_g3, ffn_b3,
                           i, alpha, seq, tm, tn_ff, tn_ff_last)

    return xf.reshape(batch, seq, d)
```
